```python
import jax, jax.numpy as jnp
from jax import lax
import numpy as np

D_MODEL = 1024
BATCH = 16
SEQ = 4096
DEPTH = 4

EPS = 1e-6
SSD_EXPAND = 2
SSD_D_INNER = SSD_EXPAND * D_MODEL
SSD_HEAD_DIM = 64
SSD_N_HEADS = SSD_D_INNER // SSD_HEAD_DIM
SSD_N_GROUPS = 4
SSD_D_STATE = 128
SSD_GN = SSD_N_GROUPS * SSD_D_STATE
SSD_CONV_DIM = SSD_D_INNER + 2 * SSD_GN
SSD_CONV_WIDTH = 5
SSD_CHUNK = 128
DT_MIN = 0.001
DT_MAX = 0.1
ATTN_N_HEADS = 16
ATTN_N_KV_HEADS = 4
ATTN_HEAD_DIM = 64
ATTN_Q_DIM = ATTN_N_HEADS * ATTN_HEAD_DIM
ATTN_KV_DIM = ATTN_N_KV_HEADS * ATTN_HEAD_DIM
ATTN_WINDOW = 128
ATTN_BLOCK = 128
D_FF = -(-8 * D_MODEL // (3 * 256)) * 256
IN_SIZES = (SSD_D_INNER, SSD_CONV_DIM, SSD_N_HEADS, SSD_N_HEADS,
            ATTN_Q_DIM, ATTN_KV_DIM, ATTN_KV_DIM, D_MODEL, D_MODEL)
IN_PROJ_DIM = sum(IN_SIZES)

kernel_name = 'hybrid_ssd_swa_gated_encoder'


def _split_cols(t, sizes):
    out, start = [], 0
    for s in sizes:
        out.append(t[..., start:start + s])
        start += s
    return out


def rmsnorm(x, w):
    xf = x.astype(jnp.float32)
    y = xf * lax.rsqrt(jnp.mean(xf * xf, axis=-1, keepdims=True) + EPS)
    return (y * w.astype(jnp.float32)).astype(x.dtype)


def depthwise_conv(x, w, b):
    K, C = w.shape
    y = lax.conv_general_dilated(x, w[:, None, :].astype(x.dtype), window_strides=(1,),
                                 padding=[(K // 2, K // 2)],
                                 dimension_numbers=('NWC', 'WIO', 'NWC'),
                                 feature_group_count=C)
    return y + b


def ssd_scan(xs, dt, A, Bm, Cm):
    Bsz, L, H, P = xs.shape
    G, N = Bm.shape[2], Bm.shape[3]
    R = H // G
    Q = SSD_CHUNK
    nc = L // Q
    xdt = (xs * dt[..., None]).reshape(Bsz, nc, Q, G, R, P)
    a = (dt.astype(jnp.float32) * A.astype(jnp.float32)).reshape(Bsz, nc, Q, G, R)
    Bc = Bm.reshape(Bsz, nc, Q, G, N)
    Cc = Cm.reshape(Bsz, nc, Q, G, N)
    a_cum = jnp.cumsum(a, axis=2)
    a_cum_t = jnp.moveaxis(a_cum, 2, -1)
    seg = a_cum_t[..., :, None] - a_cum_t[..., None, :]
    lower = jnp.tril(jnp.ones((Q, Q), dtype=bool))
    decay_in = jnp.exp(jnp.where(lower, seg, -jnp.inf))
    CB = jnp.einsum('bclgn,bcsgn->bcgls', Cc, Bc)
    y_diag = jnp.einsum('bcgls,bcgrls,bcsgrp->bclgrp', CB, decay_in, xdt)
    decay_to_end = jnp.exp(a_cum[:, :, -1:] - a_cum)
    states = jnp.einsum('bcsgn,bcsgr,bcsgrp->bcgrpn', Bc, decay_to_end, xdt)
    chunk_decay = jnp.exp(a_cum[:, :, -1])

    def step(h, inp):
        s_c, d_c = inp
        return h * d_c[..., None, None] + s_c, h

    h0 = jnp.zeros_like(states[:, 0])
    _, states_in = lax.scan(step, h0, (jnp.moveaxis(states, 1, 0), jnp.moveaxis(chunk_decay, 1, 0)))
    states_in = jnp.moveaxis(states_in, 0, 1)
    y_off = jnp.einsum('bclgn,bcgrpn,bclgr->bclgrp', Cc, states_in, jnp.exp(a_cum))
    return (y_diag + y_off).reshape(Bsz, L, H, P)


def bidirectional_ssd(xs, dt_f, dt_b, A_f, A_b, Bm, Cm):
    flip = lambda t: jnp.flip(t, axis=1)
    y_f = ssd_scan(xs, dt_f, A_f, Bm, Cm)
    y_b = flip(ssd_scan(flip(xs), flip(dt_b), A_b, flip(Bm), flip(Cm)))
    return y_f + y_b


def windowed_gqa(q, k, v, sink):
    Bsz, L, H, Dh = q.shape
    KV = k.shape[2]
    R = H // KV
    W = ATTN_BLOCK
    nb = L // W
    qb = q.reshape(Bsz, nb, W, KV, R, Dh)
    pad = ((0, 0), (W, W), (0, 0), (0, 0))
    kp = jnp.pad(k, pad).reshape(Bsz, nb + 2, W, KV, Dh)
    vp = jnp.pad(v, pad).reshape(Bsz, nb + 2, W, KV, Dh)
    kwin = jnp.concatenate([kp[:, :-2], kp[:, 1:-1], kp[:, 2:]], axis=2)
    vwin = jnp.concatenate([vp[:, :-2], vp[:, 1:-1], vp[:, 2:]], axis=2)
    scores = jnp.einsum('bnqkrd,bnskd->bnkrqs', qb, kwin).astype(jnp.float32) * (Dh ** -0.5)
    blk = jnp.arange(nb)[:, None] * W
    q_pos = blk + jnp.arange(W)[None, :]
    k_pos = blk - W + jnp.arange(3 * W)[None, :]
    dist = jnp.abs(q_pos[:, :, None] - k_pos[:, None, :])
    valid = (dist <= ATTN_WINDOW) & (k_pos[:, None, :] >= 0) & (k_pos[:, None, :] < L)
    slopes = jnp.exp2(-8.0 * jnp.arange(1, H + 1, dtype=jnp.float32) / H).reshape(KV, R)
    bias = -slopes[None, :, :, None, None] * dist[:, None, None].astype(jnp.float32)
    scores = jnp.where(valid[:, None, None][None], scores + bias[None], -jnp.inf)
    sink_l = sink.astype(jnp.float32).reshape(KV, R)[None, None, :, :, None]
    m = jnp.maximum(jnp.max(scores, axis=-1), sink_l)
    p = jnp.exp(scores - m[..., None])
    denom = jnp.sum(p, axis=-1) + jnp.exp(sink_l - m)
    out = jnp.einsum('bnkrqs,bnskd->bnqkrd', p, vwin.astype(jnp.float32))
    out = out / jnp.moveaxis(denom, -1, 2)[..., None]
    return out.reshape(Bsz, L, H * Dh).astype(q.dtype)


def setup_inputs(seed: int = 0) -> dict:
    key = jax.random.key(seed)
    ks = jax.random.split(key, 20)
    f32 = jnp.float32
    nrm = lambda k, shape, s: jax.random.normal(k, shape, f32) * s
    gain = lambda k, n: 1.0 + 0.05 * jax.random.normal(k, (DEPTH, n), f32)

    def dt_bias(k):
        u = jax.random.uniform(k, (DEPTH, SSD_N_HEADS), f32)
        dt = jnp.exp(u * (np.log(DT_MAX) - np.log(DT_MIN)) + np.log(DT_MIN))
        return dt + jnp.log(-jnp.expm1(-dt))

    def a_log(k):
        return jnp.log(jax.random.uniform(k, (DEPTH, SSD_N_HEADS), f32, 1.0, 16.0))

    return {
        'x': jax.random.normal(ks[0], (BATCH, SEQ, D_MODEL), f32),
        'pre_mix_norm': gain(ks[1], D_MODEL),
        'w_in': nrm(ks[2], (DEPTH, D_MODEL, IN_PROJ_DIM), D_MODEL ** -0.5),
        'conv_w': nrm(ks[3], (DEPTH, SSD_CONV_WIDTH, SSD_CONV_DIM), SSD_CONV_WIDTH ** -0.5),
        'conv_b': nrm(ks[4], (DEPTH, SSD_CONV_DIM), 0.02),
        'dt_bias_f': dt_bias(ks[5]),
        'dt_bias_b': dt_bias(ks[6]),
        'A_log_f': a_log(ks[7]),
        'A_log_b': a_log(ks[8]),
        'D_skip': 1.0 + 0.1 * jax.random.normal(ks[9], (DEPTH, SSD_N_HEADS), f32),
        'ssd_norm': gain(ks[10], SSD_D_INNER),
        'w_ssd_proj': nrm(ks[11], (DEPTH, SSD_D_INNER, D_MODEL), SSD_D_INNER ** -0.5),
        'attn_sink': nrm(ks[12], (DEPTH, ATTN_N_HEADS), 0.5),
        'w_attn_proj': nrm(ks[13], (DEPTH, ATTN_Q_DIM, D_MODEL), ATTN_Q_DIM ** -0.5),
        'w_out': nrm(ks[14], (DEPTH, D_MODEL, D_MODEL), D_MODEL ** -0.5),
        'post_mix_norm': gain(ks[15], D_MODEL),
        'pre_ffn_norm': gain(ks[16], D_MODEL),
        'w_ffn_in': nrm(ks[17], (DEPTH, D_MODEL, 2 * D_FF), D_MODEL ** -0.5),
        'w_ffn_out': nrm(ks[18], (DEPTH, D_FF, D_MODEL), D_FF ** -0.5),
        'post_ffn_norm': gain(ks[19], D_MODEL),
    }


def reference(x, pre_mix_norm, w_in, conv_w, conv_b, dt_bias_f, dt_bias_b, A_log_f, A_log_b,
              D_skip, ssd_norm, w_ssd_proj, attn_sink, w_attn_proj, w_out, post_mix_norm,
              pre_ffn_norm, w_ffn_in, w_ffn_out, post_ffn_norm):
    Bsz, L, _ = x.shape
    for i in range(DEPTH):
        h = rmsnorm(x, pre_mix_norm[i])
        proj = h @ w_in[i]
        z, xBC, dtf_raw, dtb_raw, q, k, v, gate_a, gate_b = _split_cols(proj, IN_SIZES)
        xBC = jax.nn.silu(depthwise_conv(xBC, conv_w[i], conv_b[i]))
        xs, Bm, Cm = _split_cols(xBC, (SSD_D_INNER, SSD_GN, SSD_GN))
        xs = xs.reshape(Bsz, L, SSD_N_HEADS, SSD_HEAD_DIM)
        Bm = Bm.reshape(Bsz, L, SSD_N_GROUPS, SSD_D_STATE)
        Cm = Cm.reshape(Bsz, L, SSD_N_GROUPS, SSD_D_STATE)
        dt_f = jax.nn.softplus(dtf_raw + dt_bias_f[i])
        dt_b = jax.nn.softplus(dtb_raw + dt_bias_b[i])
        A_f = -jnp.exp(A_log_f[i])
        A_b = -jnp.exp(A_log_b[i])
        y = bidirectional_ssd(xs, dt_f, dt_b, A_f, A_b, Bm, Cm) + D_skip[i][:, None] * xs
        y = y.reshape(Bsz, L, SSD_D_INNER).astype(x.dtype)
        y = rmsnorm(y * jax.nn.silu(z), ssd_norm[i])
        y_ssd = y @ w_ssd_proj[i]
        q = q.reshape(Bsz, L, ATTN_N_HEADS, ATTN_HEAD_DIM)
        k = k.reshape(Bsz, L, ATTN_N_KV_HEADS, ATTN_HEAD_DIM)
        v = v.reshape(Bsz, L, ATTN_N_KV_HEADS, ATTN_HEAD_DIM)
        y_attn = windowed_gqa(q, k, v, attn_sink[i]) @ w_attn_proj[i]
        mix = (jax.nn.sigmoid(gate_a) * y_ssd + jax.nn.sigmoid(gate_b) * y_attn) @ w_out[i]
        x = x + rmsnorm(mix, post_mix_norm[i])
        h2 = rmsnorm(x, pre_ffn_norm[i])
        g, u = _split_cols(h2 @ w_ffn_in[i], (D_FF, D_FF))
        f = (jax.nn.silu(g) * u) @ w_ffn_out[i]
        x = x + rmsnorm(f, post_ffn_norm[i])
    return x
```

```python
import functools

import jax
import jax.numpy as jnp
from jax import lax
from jax.experimental import pallas as pl
from jax.experimental.pallas import tpu as pltpu

F32 = jnp.float32
BF16 = jnp.bfloat16
EPS = 1e-6
NEG = -1e30

D_MODEL = 1024
D_INNER = 2048
HEAD_DIM = 64
N_HEADS = 32
N_GROUPS = 4
HEADS_PER_GROUP = N_HEADS // N_GROUPS
D_STATE = 128
GN = N_GROUPS * D_STATE
CONV_DIM = D_INNER + 2 * GN
CONV_W = 5
CHUNK = 128
A_HEADS = 16
A_KV = 4
A_DIM = 64
Q_DIM = A_HEADS * A_DIM
KV_DIM = A_KV * A_DIM
WINDOW = 128
D_FF = 2816

P_Z = 0
P_XS = P_Z + D_INNER
P_Q = P_XS + D_INNER
P_GA = P_Q + Q_DIM
P_GB = P_GA + D_MODEL
P_B = P_GB + D_MODEL
P_C = P_B + GN
P_K = P_C + GN
P_V = P_K + KV_DIM
NP = P_V + KV_DIM
DT_PAD = 128

VMEM_LIMIT_BYTES = 56 * 1024 * 1024
LANES = 128


def _rms(x, w):
    ms = jnp.mean(x * x, axis=-1, keepdims=True)
    return x * lax.rsqrt(ms + EPS) * w


def _softplus(x):
    return jnp.maximum(x, 0.0) + jnp.log1p(jnp.exp(-jnp.abs(x)))


def _inproj_body(x_ref, nw_ref, w_ref, wdt_ref, p_ref, dt_ref, h_scr):
    @pl.when(pl.program_id(1) == 0)
    def _():
        h = _rms(x_ref[...], nw_ref[...]).astype(BF16)
        h_scr[...] = h
        dt_ref[...] = jnp.dot(h, wdt_ref[...], preferred_element_type=F32)

    p_ref[...] = jnp.dot(h_scr[...], w_ref[...], preferred_element_type=F32).astype(BF16)


def _in_proj(x2, norm_w, w_main, w_dt, layer, tm, tn):
    T = x2.shape[0]
    return pl.pallas_call(
        _inproj_body,
        grid=(T // tm, NP // tn),
        in_specs=[
            pl.BlockSpec((tm, D_MODEL), lambda i, j: (i, 0)),
            pl.BlockSpec((None, 1, D_MODEL), lambda i, j: (layer, 0, 0)),
            pl.BlockSpec((None, D_MODEL, tn), lambda i, j: (layer, 0, j)),
            pl.BlockSpec((None, D_MODEL, DT_PAD), lambda i, j: (layer, 0, 0)),
        ],
        out_specs=[
            pl.BlockSpec((tm, tn), lambda i, j: (i, j)),
            pl.BlockSpec((tm, DT_PAD), lambda i, j: (i, 0)),
        ],
        out_shape=[
            jax.ShapeDtypeStruct((T, NP), BF16),
            jax.ShapeDtypeStruct((T, DT_PAD), F32),
        ],
        scratch_shapes=[pltpu.VMEM((tm, D_MODEL), BF16)],
        compiler_params=pltpu.CompilerParams(
            dimension_semantics=("parallel", "arbitrary"),
            vmem_limit_bytes=VMEM_LIMIT_BYTES),
        name="in_proj",
    )(x2, norm_w, w_main, w_dt)


HALO = 16


def _conv_silu_chunk(src_ref, w_ref, b_ref, dst_ref, buf_ref, c, nc, width):
    r0 = pl.multiple_of(c * CHUNK, CHUNK)
    prev0 = pl.multiple_of(jnp.maximum(r0 - HALO, 0), HALO)
    next0 = pl.multiple_of(jnp.minimum(r0 + CHUNK, (nc - 1) * CHUNK + CHUNK - HALO), HALO)
    prev = src_ref[0, pl.ds(prev0, HALO), :].astype(F32)
    nxt = src_ref[0, pl.ds(next0, HALO), :].astype(F32)
    prev = jnp.where(c > 0, prev, 0.0)
    nxt = jnp.where(c < nc - 1, nxt, 0.0)
    buf_ref[0:HALO, 0:width] = prev
    buf_ref[HALO:HALO + CHUNK, 0:width] = src_ref[0, pl.ds(r0, CHUNK), :].astype(F32)
    buf_ref[HALO + CHUNK:2 * HALO + CHUNK, 0:width] = nxt
    acc = jnp.zeros((CHUNK, width), F32) + b_ref[...]
    for k in range(CONV_W):
        off = HALO - CONV_W // 2 + k
        acc = acc + buf_ref[off:off + CHUNK, 0:width] * w_ref[k:k + 1, :]
    dst_ref[pl.ds(r0, CHUNK), :] = (acc * jax.nn.sigmoid(acc)).astype(BF16)


def _lane_cumsum(x, reverse):
    lane = lax.broadcasted_iota(jnp.int32, x.shape, 1)
    k = 1
    while k < LANES:
        if reverse:
            sh = pltpu.roll(x, LANES - k, axis=1)
            x = x + jnp.where(lane < LANES - k, sh, 0.0)
        else:
            sh = pltpu.roll(x, k, axis=1)
            x = x + jnp.where(lane >= k, sh, 0.0)
        k *= 2
    return x


def _ssd_chunk(c, reverse, refs):
    (dt_ref, hp_ref, xs_s, b_s, c_s, dtT_s, aT_s, h_s) = refs
    g = pl.program_id(1)
    r0 = pl.multiple_of(c * CHUNK, CHUNK)

    dt_tok = _softplus(dt_ref[0, pl.ds(r0, CHUNK), :] + hp_ref[0:1, :])
    a_tok = dt_tok * (-jnp.exp(hp_ref[1:2, :]))
    dtT_s[...] = dt_tok.T
    aT_s[...] = a_tok.T
    hrow = pl.multiple_of((N_HEADS if reverse else 0) + g * HEADS_PER_GROUP, HEADS_PER_GROUP)
    dt8 = dtT_s[pl.ds(hrow, HEADS_PER_GROUP), :]
    cum8 = _lane_cumsum(aT_s[pl.ds(hrow, HEADS_PER_GROUP), :], reverse)
    end8 = cum8[:, 0:1] if reverse else cum8[:, LANES - 1:LANES]
    w8 = dt8 * jnp.exp(end8 - cum8)
    cd8 = jnp.exp(jnp.broadcast_to(end8, (HEADS_PER_GROUP, LANES)))

    xs = xs_s[pl.ds(r0, CHUNK), :]
    bc = b_s[pl.ds(r0, CHUNK), :]
    cc = c_s[pl.ds(r0, CHUNK), :]
    cb = lax.dot_general(cc, bc, (((1,), (1,)), ((), ())), preferred_element_type=F32)
    cf = cc.astype(F32)
    bT = bc.astype(F32).T

    li = lax.broadcasted_iota(jnp.int32, (CHUNK, CHUNK), 0)
    si = lax.broadcasted_iota(jnp.int32, (CHUNK, CHUNK), 1)
    mask = (si >= li) if reverse else (si <= li)
    lane = lax.broadcasted_iota(jnp.int32, (CHUNK, LANES), 1)
    low = lane < HEAD_DIM

    y_parts = []
    for p in range(HEADS_PER_GROUP // 2):
        xp = xs[:, p * LANES:(p + 1) * LANES]
        hp_state = h_s[:, p * LANES:(p + 1) * LANES]
        rhs = jnp.concatenate([xp, hp_state.astype(BF16)], axis=0)
        res, st = [], []
        for e in range(2):
            j = 2 * p + e
            arow = jnp.broadcast_to(cum8[j:j + 1, :], (CHUNK, CHUNK))
            acol = arow.T
            dec = jnp.exp(jnp.where(mask, acol - arow, NEG))
            m = cb * dec * dt8[j:j + 1, :]
            lhs = jnp.concatenate([m.astype(BF16), (cf * jnp.exp(acol)).astype(BF16)], axis=1)
            res.append(jnp.dot(lhs, rhs, preferred_element_type=F32))
            btw = (bT * w8[j:j + 1, :]).astype(BF16)
            st.append(jnp.dot(btw, xp, preferred_element_type=F32))
        y_parts.append(jnp.where(low, res[0], res[1]))
        cd_pair = jnp.where(low[0:1, :], cd8[2 * p:2 * p + 1, :], cd8[2 * p + 1:2 * p + 2, :])
        h_s[:, p * LANES:(p + 1) * LANES] = hp_state * cd_pair + jnp.where(low, st[0], st[1])
    return jnp.concatenate(y_parts, axis=1)


def _ssd_body(xs_ref, b_ref, c_ref, dt_ref, cwx_ref, cwb_ref, cwc_ref, cbx_ref, cbb_ref, cbc_ref,
              hp_ref, dsk_ref, y_ref,
              xs_s, b_s, c_s, buf_s, dtT_s, aT_s, yacc_s, h_s, *, nc):
    gw = HEADS_PER_GROUP * HEAD_DIM

    def conv_step(c, carry):
        _conv_silu_chunk(xs_ref, cwx_ref, cbx_ref, xs_s, buf_s, c, nc, gw)
        _conv_silu_chunk(b_ref, cwb_ref, cbb_ref, b_s, buf_s, c, nc, D_STATE)
        _conv_silu_chunk(c_ref, cwc_ref, cbc_ref, c_s, buf_s, c, nc, D_STATE)
        return carry

    lax.fori_loop(0, nc, conv_step, 0)

    refs = (dt_ref, hp_ref, xs_s, b_s, c_s, dtT_s, aT_s, h_s)

    h_s[...] = jnp.zeros_like(h_s)

    def fwd_step(c, carry):
        r0 = pl.multiple_of(c * CHUNK, CHUNK)
        y = _ssd_chunk(c, False, refs)
        yacc_s[pl.ds(r0, CHUNK), :] = y + dsk_ref[...] * xs_s[pl.ds(r0, CHUNK), :].astype(F32)
        return carry

    lax.fori_loop(0, nc, fwd_step, 0)

    h_s[...] = jnp.zeros_like(h_s)

    def bwd_step(t, carry):
        c = nc - 1 - t
        r0 = pl.multiple_of(c * CHUNK, CHUNK)
        y = _ssd_chunk(c, True, refs)
        y_ref[0, pl.ds(r0, CHUNK), :] = (yacc_s[pl.ds(r0, CHUNK), :] + y).astype(BF16)
        return carry

    lax.fori_loop(0, nc, bwd_step, 0)


def _ssd(p3, dt3, conv_w, conv_b, head_params, dskip, layer):
    Bsz, L, _ = p3.shape
    nc = L // CHUNK
    gw = HEADS_PER_GROUP * HEAD_DIM
    xs_blk, b_blk, c_blk = P_XS // gw, P_B // D_STATE, P_C // D_STATE
    cw_b0, cw_c0 = D_INNER // D_STATE, (D_INNER + GN) // D_STATE
    return pl.pallas_call(
        functools.partial(_ssd_body, nc=nc),
        grid=(Bsz, N_GROUPS),
        in_specs=[
            pl.BlockSpec((1, L, gw), lambda b, g: (b, 0, xs_blk + g)),
            pl.BlockSpec((1, L, D_STATE), lambda b, g: (b, 0, b_blk + g)),
            pl.BlockSpec((1, L, D_STATE), lambda b, g: (b, 0, c_blk + g)),
            pl.BlockSpec((1, L, DT_PAD), lambda b, g: (b, 0, 0)),
            pl.BlockSpec((None, CONV_W, gw), lambda b, g: (layer, 0, g)),
            pl.BlockSpec((None, CONV_W, D_STATE), lambda b, g: (layer, 0, cw_b0 + g)),
            pl.BlockSpec((None, CONV_W, D_STATE), lambda b, g: (layer, 0, cw_c0 + g)),
            pl.BlockSpec((None, 1, gw), lambda b, g: (layer, 0, g)),
            pl.BlockSpec((None, 1, D_STATE), lambda b, g: (layer, 0, cw_b0 + g)),
            pl.BlockSpec((None, 1, D_STATE), lambda b, g: (layer, 0, cw_c0 + g)),
            pl.BlockSpec((None, 8, LANES), lambda b, g: (layer, 0, 0)),
            pl.BlockSpec((None, 1, gw), lambda b, g: (layer, 0, g)),
        ],
        out_specs=pl.BlockSpec((1, L, gw), lambda b, g: (b, 0, g)),
        out_shape=jax.ShapeDtypeStruct((Bsz, L, D_INNER), BF16),
        scratch_shapes=[
            pltpu.VMEM((L, gw), BF16),
            pltpu.VMEM((L, D_STATE), BF16),
            pltpu.VMEM((L, D_STATE), BF16),
            pltpu.VMEM((CHUNK + 2 * HALO, gw), F32),
            pltpu.VMEM((LANES, CHUNK), F32),
            pltpu.VMEM((LANES, CHUNK), F32),
            pltpu.VMEM((L, gw), F32),
            pltpu.VMEM((D_STATE, gw), F32),
        ],
        compiler_params=pltpu.CompilerParams(
            dimension_semantics=("parallel", "parallel"),
            vmem_limit_bytes=VMEM_LIMIT_BYTES),
        name="ssd",
    )(p3, p3, p3, dt3, conv_w, conv_w, conv_w, conv_b, conv_b, conv_b, head_params, dskip)


def _attn_body(sink_ref, q_ref, k_ref, v_ref, o_ref, *, layer, nb):
    i = pl.program_id(1)
    kv_rows = 3 * WINDOW
    start = pl.multiple_of(jnp.clip(i - 1, 0, nb - 3) * WINDOW, WINDOW)
    kw = k_ref[0, pl.ds(start, kv_rows), :]
    vw = v_ref[0, pl.ds(start, kv_rows), :]
    off = i * WINDOW - start
    r = lax.broadcasted_iota(jnp.int32, (WINDOW, kv_rows), 0)
    c = lax.broadcasted_iota(jnp.int32, (WINDOW, kv_rows), 1)
    dist = jnp.abs(off + r - c)
    valid = dist <= WINDOW
    distf = dist.astype(F32)
    scale = A_DIM ** -0.5
    rep = A_HEADS // A_KV
    for g in range(A_KV):
        kg = kw[:, g * A_DIM:(g + 1) * A_DIM]
        vg = vw[:, g * A_DIM:(g + 1) * A_DIM]
        for e in range(rep):
            h = g * rep + e
            slope = 2.0 ** (-8.0 * (h + 1) / A_HEADS)
            qh = q_ref[0, :, h * A_DIM:(h + 1) * A_DIM]
            s = lax.dot_general(qh, kg, (((1,), (1,)), ((), ())), preferred_element_type=F32)
            s = jnp.where(valid, s * scale - slope * distf, NEG)
            sink = sink_ref[layer, h]
            m = jnp.maximum(jnp.max(s, axis=-1, keepdims=True), sink)
            p = jnp.exp(s - m)
            denom = jnp.sum(p, axis=-1, keepdims=True) + jnp.exp(sink - m)
            o = jnp.dot(p.astype(BF16), vg, preferred_element_type=F32) / denom
            o_ref[0, :, h * A_DIM:(h + 1) * A_DIM] = o.astype(BF16)


def _attn(p3, sink, layer):
    Bsz, L, _ = p3.shape
    nb = L // WINDOW
    return pl.pallas_call(
        functools.partial(_attn_body, layer=layer, nb=nb),
        grid=(Bsz, nb),
        in_specs=[
            pl.BlockSpec(memory_space=pltpu.SMEM),
            pl.BlockSpec((1, WINDOW, Q_DIM), lambda b, i: (b, i, P_Q // Q_DIM)),
            pl.BlockSpec((1, L, KV_DIM), lambda b, i: (b, 0, P_K // KV_DIM)),
            pl.BlockSpec((1, L, KV_DIM), lambda b, i: (b, 0, P_V // KV_DIM)),
        ],
        out_specs=pl.BlockSpec((1, WINDOW, Q_DIM), lambda b, i: (b, i, 0)),
        out_shape=jax.ShapeDtypeStruct((Bsz, L, Q_DIM), BF16),
        compiler_params=pltpu.CompilerParams(
            dimension_semantics=("parallel", "arbitrary"),
            vmem_limit_bytes=VMEM_LIMIT_BYTES),
        name="attn",
    )(sink, p3, p3, p3)


def _mix_body(y_ref, z_ref, at_ref, ga_ref, gb_ref, x_ref, nssd_ref, npost_ref,
              wssd_ref, wat_ref, wout_ref, o_ref):
    u = y_ref[...].astype(F32) * jax.nn.silu(z_ref[...].astype(F32))
    un = _rms(u, nssd_ref[...]).astype(BF16)
    y_ssd = jnp.dot(un, wssd_ref[...], preferred_element_type=F32)
    y_at = jnp.dot(at_ref[...], wat_ref[...], preferred_element_type=F32)
    mix_in = (jax.nn.sigmoid(ga_ref[...].astype(F32)) * y_ssd
              + jax.nn.sigmoid(gb_ref[...].astype(F32)) * y_at).astype(BF16)
    mix = jnp.dot(mix_in, wout_ref[...], preferred_element_type=F32)
    o_ref[...] = x_ref[...] + _rms(mix, npost_ref[...])


def _mix(y2, p2, at2, x2, n_ssd, n_post, w_ssd, w_at, w_out, layer, tm):
    T = x2.shape[0]
    row = lambda blk: (lambda i: (i, blk))
    par = lambda i: (layer, 0, 0)
    return pl.pallas_call(
        _mix_body,
        grid=(T // tm,),
        in_specs=[
            pl.BlockSpec((tm, D_INNER), row(0)),
            pl.BlockSpec((tm, D_INNER), row(P_Z // D_INNER)),
            pl.BlockSpec((tm, Q_DIM), row(0)),
            pl.BlockSpec((tm, D_MODEL), row(P_GA // D_MODEL)),
            pl.BlockSpec((tm, D_MODEL), row(P_GB // D_MODEL)),
            pl.BlockSpec((tm, D_MODEL), row(0)),
            pl.BlockSpec((None, 1, D_INNER), par),
            pl.BlockSpec((None, 1, D_MODEL), par),
            pl.BlockSpec((None, D_INNER, D_MODEL), par),
            pl.BlockSpec((None, Q_DIM, D_MODEL), par),
            pl.BlockSpec((None, D_MODEL, D_MODEL), par),
        ],
        out_specs=pl.BlockSpec((tm, D_MODEL), row(0)),
        out_shape=jax.ShapeDtypeStruct((T, D_MODEL), F32),
        compiler_params=pltpu.CompilerParams(
            dimension_semantics=("parallel",),
            vmem_limit_bytes=VMEM_LIMIT_BYTES),
        name="mix",
    )(y2, p2, at2, p2, p2, x2, n_ssd, n_post, w_ssd, w_at, w_out)


FF_CHUNK = D_FF // 2


def _ffn_body(x_ref, npre_ref, npost_ref, win_ref, wout_ref, o_ref):
    x = x_ref[...]
    h = _rms(x, npre_ref[...]).astype(BF16)
    acc = jnp.zeros(x.shape, F32)
    for c0 in range(0, D_FF, FF_CHUNK):
        gate = jnp.dot(h, win_ref[:, c0:c0 + FF_CHUNK], preferred_element_type=F32)
        up = jnp.dot(h, win_ref[:, D_FF + c0:D_FF + c0 + FF_CHUNK], preferred_element_type=F32)
        act = (jax.nn.silu(gate) * up).astype(BF16)
        acc = acc + jnp.dot(act, wout_ref[c0:c0 + FF_CHUNK, :], preferred_element_type=F32)
    o_ref[...] = x + _rms(acc, npost_ref[...])


def _ffn(x2, n_pre, n_post, w_in, w_out, layer, tm):
    T = x2.shape[0]
    par = lambda i: (layer, 0, 0)
    return pl.pallas_call(
        _ffn_body,
        grid=(T // tm,),
        in_specs=[
            pl.BlockSpec((tm, D_MODEL), lambda i: (i, 0)),
            pl.BlockSpec((None, 1, D_MODEL), par),
            pl.BlockSpec((None, 1, D_MODEL), par),
            pl.BlockSpec((None, D_MODEL, 2 * D_FF), par),
            pl.BlockSpec((None, D_FF, D_MODEL), par),
        ],
        out_specs=pl.BlockSpec((tm, D_MODEL), lambda i: (i, 0)),
        out_shape=jax.ShapeDtypeStruct((T, D_MODEL), F32),
        compiler_params=pltpu.CompilerParams(
            dimension_semantics=("parallel",),
            vmem_limit_bytes=VMEM_LIMIT_BYTES),
        name="ffn",
    )(x2, n_pre, n_post, w_in, w_out)


def _tile(total, want):
    t = min(total, want)
    assert total % t == 0, (total, want)
    return t


def kernel(x, pre_mix_norm, w_in, conv_w, conv_b, dt_bias_f, dt_bias_b, A_log_f, A_log_b, D_skip,
           ssd_norm, w_ssd_proj, attn_sink, w_attn_proj, w_out, post_mix_norm, pre_ffn_norm,
           w_ffn_in, w_ffn_out, post_ffn_norm):
    Bsz, L, D = x.shape
    depth = w_in.shape[0]
    assert D == D_MODEL and L % CHUNK == 0 and L >= 3 * WINDOW
    T = Bsz * L

    sizes = dict(z=D_INNER, xs=D_INNER, b=GN, c=GN, dt=2 * N_HEADS, q=Q_DIM, k=KV_DIM, v=KV_DIM,
                 ga=D_MODEL, gb=D_MODEL)
    cols, start = {}, 0
    for name, size in sizes.items():
        cols[name] = w_in[:, :, start:start + size]
        start += size
    assert start == w_in.shape[-1]
    w_main = jnp.concatenate(
        [cols[n] for n in ("z", "xs", "q", "ga", "gb", "b", "c", "k", "v")], axis=-1).astype(BF16)
    assert w_main.shape[-1] == NP
    w_dt = jnp.pad(cols["dt"], ((0, 0), (0, 0), (0, DT_PAD - 2 * N_HEADS))).astype(BF16)
    lane_pad = ((0, 0), (0, LANES - 2 * N_HEADS))
    head_params = jnp.stack(
        [jnp.pad(jnp.concatenate([dt_bias_f, dt_bias_b], axis=-1), lane_pad),
         jnp.pad(jnp.concatenate([A_log_f, A_log_b], axis=-1), lane_pad)]
        + [jnp.zeros((depth, LANES), F32)] * 6, axis=1)
    dskip = jnp.repeat(D_skip, HEAD_DIM, axis=-1)[:, None, :]
    vec = lambda a: a[:, None, :]
    conv_b3 = vec(conv_b)
    w_ssd_b, w_at_b, w_out_b = (w.astype(BF16) for w in (w_ssd_proj, w_attn_proj, w_out))
    w_fin_b, w_fout_b = w_ffn_in.astype(BF16), w_ffn_out.astype(BF16)

    tm_in = _tile(T, 1024)
    tm_mix = _tile(T, 512)
    tm_ffn = _tile(T, 512)

    x2 = x.reshape(T, D)
    for i in range(depth):
        p2, dt2 = _in_proj(x2, vec(pre_mix_norm), w_main, w_dt, i, tm_in, 512)
        p3 = p2.reshape(Bsz, L, NP)
        y3 = _ssd(p3, dt2.reshape(Bsz, L, DT_PAD), conv_w, conv_b3, head_params, dskip, i)
        at3 = _attn(p3, attn_sink, i)
        x2 = _mix(y3.reshape(T, D_INNER), p2, at3.reshape(T, Q_DIM), x2, vec(ssd_norm),
                  vec(post_mix_norm), w_ssd_b, w_at_b, w_out_b, i, tm_mix)
        x2 = _ffn(x2, vec(pre_ffn_norm), vec(post_ffn_norm), w_fin_b, w_fout_b, i, tm_ffn)
    return x2.reshape(Bsz, L, D)
```

```python
import functools

import jax
import jax.numpy as jnp
from jax import lax
from jax.experimental import pallas as pl
from jax.experimental.pallas import tpu as pltpu

F32 = jnp.float32
BF16 = jnp.bfloat16
EPS = 1e-6
NEG = -1e30

D_MODEL = 1024
D_INNER = 2048
HEAD_DIM = 64
N_HEADS = 32
N_GROUPS = 4
HEADS_PER_GROUP = N_HEADS // N_GROUPS
D_STATE = 128
GN = N_GROUPS * D_STATE
CONV_DIM = D_INNER + 2 * GN
CONV_W = 5
CHUNK = 128
A_HEADS = 16
A_KV = 4
A_DIM = 64
Q_DIM = A_HEADS * A_DIM
KV_DIM = A_KV * A_DIM
WINDOW = 128
D_FF = 2816

P_Z = 0
P_XS = P_Z + D_INNER
P_Q = P_XS + D_INNER
P_GA = P_Q + Q_DIM
P_GB = P_GA + D_MODEL
P_B = P_GB + D_MODEL
P_C = P_B + GN
P_K = P_C + GN
P_V = P_K + 2 * KV_DIM
NP = P_V + 2 * KV_DIM
DT_PAD = 128

VMEM_LIMIT_BYTES = 56 * 1024 * 1024
LANES = 128


def _rms(x, w):
    ms = jnp.mean(x * x, axis=-1, keepdims=True)
    return x * lax.rsqrt(ms + EPS) * w


def _softplus(x):
    return jnp.maximum(x, 0.0) + jnp.log1p(jnp.exp(-jnp.abs(x)))


def _inproj_body(x_ref, nw_ref, w_ref, wdt_ref, hp_ref, p_ref, dt_ref, h_scr):
    @pl.when(pl.program_id(1) == 0)
    def _():
        h = _rms(x_ref[...], nw_ref[...]).astype(BF16)
        h_scr[...] = h
        raw = jnp.dot(h, wdt_ref[...], preferred_element_type=F32)
        dt = _softplus(raw + hp_ref[0:1, :])
        lane = lax.broadcasted_iota(jnp.int32, (1, DT_PAD), 1)
        dt_ref[...] = dt * jnp.where(lane < 2 * N_HEADS, 1.0, -jnp.exp(hp_ref[1:2, :]))

    p_ref[...] = jnp.dot(h_scr[...], w_ref[...], preferred_element_type=F32).astype(BF16)


def _in_proj(x2, norm_w, w_main, w_dt, head_params, layer, tm, tn):
    T = x2.shape[0]
    return pl.pallas_call(
        _inproj_body,
        grid=(T // tm, NP // tn),
        in_specs=[
            pl.BlockSpec((tm, D_MODEL), lambda i, j: (i, 0)),
            pl.BlockSpec((None, 1, D_MODEL), lambda i, j: (layer, 0, 0)),
            pl.BlockSpec((None, D_MODEL, tn), lambda i, j: (layer, 0, j)),
            pl.BlockSpec((None, D_MODEL, DT_PAD), lambda i, j: (layer, 0, 0)),
            pl.BlockSpec((None, 8, LANES), lambda i, j: (layer, 0, 0)),
        ],
        out_specs=[
            pl.BlockSpec((tm, tn), lambda i, j: (i, j)),
            pl.BlockSpec((tm, DT_PAD), lambda i, j: (i, 0)),
        ],
        out_shape=[
            jax.ShapeDtypeStruct((T, NP), BF16),
            jax.ShapeDtypeStruct((T, DT_PAD), F32),
        ],
        scratch_shapes=[pltpu.VMEM((tm, D_MODEL), BF16)],
        compiler_params=pltpu.CompilerParams(
            dimension_semantics=("parallel", "arbitrary"),
            vmem_limit_bytes=VMEM_LIMIT_BYTES),
        name="in_proj",
    )(x2, norm_w, w_main, w_dt, head_params)


HALO = 16


def _conv_silu_chunk(src_ref, w_ref, b_ref, dst_ref, buf_ref, c, nc, width):
    r0 = pl.multiple_of(c * CHUNK, CHUNK)
    prev0 = pl.multiple_of(jnp.maximum(r0 - HALO, 0), HALO)
    next0 = pl.multiple_of(jnp.minimum(r0 + CHUNK, (nc - 1) * CHUNK + CHUNK - HALO), HALO)
    prev = src_ref[0, pl.ds(prev0, HALO), :].astype(F32)
    nxt = src_ref[0, pl.ds(next0, HALO), :].astype(F32)
    prev = jnp.where(c > 0, prev, 0.0)
    nxt = jnp.where(c < nc - 1, nxt, 0.0)
    buf_ref[0:HALO, 0:width] = prev
    buf_ref[HALO:HALO + CHUNK, 0:width] = src_ref[0, pl.ds(r0, CHUNK), :].astype(F32)
    buf_ref[HALO + CHUNK:2 * HALO + CHUNK, 0:width] = nxt
    acc = jnp.zeros((CHUNK, width), F32) + b_ref[...]
    for k in range(CONV_W):
        off = HALO - CONV_W // 2 + k
        acc = acc + buf_ref[off:off + CHUNK, 0:width] * w_ref[k:k + 1, :]
    dst_ref[pl.ds(r0, CHUNK), :] = (acc * jax.nn.sigmoid(acc)).astype(BF16)


def _lane_cumsum(x, reverse):
    n = x.shape[1]
    lane = lax.broadcasted_iota(jnp.int32, x.shape, 1) & (LANES - 1)
    k = 1
    while k < LANES:
        if reverse:
            sh = pltpu.roll(x, n - k, axis=1)
            x = x + jnp.where(lane < LANES - k, sh, 0.0)
        else:
            sh = pltpu.roll(x, k, axis=1)
            x = x + jnp.where(lane >= k, sh, 0.0)
        k *= 2
    return x


ROW_CUM, ROW_SRC, ROW_W, ROW_CD, ROWS_PER_DIR = 0, 8, 16, 24, 32


def _scan_tables(dta_ref, hm_s, pre_s, nc):
    def transpose_step(c, carry):
        r0 = pl.multiple_of(c * CHUNK, CHUNK)
        hm_s[:, pl.ds(r0, CHUNK)] = dta_ref[0, pl.ds(r0, CHUNK), :].T
        return carry

    lax.fori_loop(0, nc, transpose_step, 0)

    g8 = pl.program_id(1) * HEADS_PER_GROUP
    for d in range(2):
        dt = hm_s[pl.ds(pl.multiple_of(d * N_HEADS + g8, 8), 8), :]
        a = hm_s[pl.ds(pl.multiple_of((2 + d) * N_HEADS + g8, 8), 8), :]
        inc = _lane_cumsum(a, False)
        rev = _lane_cumsum(a, True)
        cum, other = (inc, rev) if d == 0 else (rev, inc)
        base = d * ROWS_PER_DIR
        pre_s[base + ROW_CUM:base + ROW_CUM + 8, :] = cum
        pre_s[base + ROW_SRC:base + ROW_SRC + 8, :] = cum - jnp.log(dt)
        pre_s[base + ROW_W:base + ROW_W + 8, :] = dt * jnp.exp(other - a)
        pre_s[base + ROW_CD:base + ROW_CD + 8, :] = jnp.exp(inc + rev - a)


def _ssd_chunk(c, reverse, refs):
    (pre_s, xs_s, b_s, c_s, h_s) = refs
    r0 = pl.multiple_of(c * CHUNK, CHUNK)
    base = ROWS_PER_DIR if reverse else 0
    cum8 = pre_s[base + ROW_CUM:base + ROW_CUM + 8, pl.ds(r0, CHUNK)]
    src8 = pre_s[base + ROW_SRC:base + ROW_SRC + 8, pl.ds(r0, CHUNK)]
    w8 = pre_s[base + ROW_W:base + ROW_W + 8, pl.ds(r0, CHUNK)]
    cd8 = pre_s[base + ROW_CD:base + ROW_CD + 8, pl.ds(r0, CHUNK)]

    xs = xs_s[pl.ds(r0, CHUNK), :]
    bc = b_s[pl.ds(r0, CHUNK), :]
    cc = c_s[pl.ds(r0, CHUNK), :]
    cb = lax.dot_general(cc, bc, (((1,), (1,)), ((), ())), preferred_element_type=F32)
    cf = cc.astype(F32)
    bT = bc.astype(F32).T

    li = lax.broadcasted_iota(jnp.int32, (CHUNK, CHUNK), 0)
    si = lax.broadcasted_iota(jnp.int32, (CHUNK, CHUNK), 1)
    mask = (si >= li) if reverse else (si <= li)
    lane = lax.broadcasted_iota(jnp.int32, (CHUNK, LANES), 1)
    low = lane < HEAD_DIM

    y_parts = []
    for p in range(HEADS_PER_GROUP // 2):
        xp = xs[:, p * LANES:(p + 1) * LANES]
        hp_state = h_s[:, p * LANES:(p + 1) * LANES]
        rhs = jnp.concatenate([xp, hp_state.astype(BF16)], axis=0)
        res, st = [], []
        for e in range(2):
            j = 2 * p + e
            acol = jnp.broadcast_to(cum8[j:j + 1, :], (CHUNK, CHUNK)).T
            m = cb * jnp.exp(jnp.where(mask, acol - src8[j:j + 1, :], NEG))
            lhs = jnp.concatenate([m.astype(BF16), (cf * jnp.exp(acol)).astype(BF16)], axis=1)
            res.append(jnp.dot(lhs, rhs, preferred_element_type=F32))
            btw = (bT * w8[j:j + 1, :]).astype(BF16)
            st.append(jnp.dot(btw, xp, preferred_element_type=F32))
        y_parts.append(jnp.where(low, res[0], res[1]))
        cd_pair = jnp.where(low[0:1, :], cd8[2 * p:2 * p + 1, :], cd8[2 * p + 1:2 * p + 2, :])
        h_s[:, p * LANES:(p + 1) * LANES] = hp_state * cd_pair + jnp.where(low, st[0], st[1])
    return jnp.concatenate(y_parts, axis=1)


def _ssd_body(xs_ref, b_ref, c_ref, dta_ref, cwx_ref, cwb_ref, cwc_ref, cbx_ref, cbb_ref, cbc_ref,
              dsk_ref, y_ref,
              xs_s, b_s, c_s, buf_s, hm_s, pre_s, yacc_s, h_s, *, nc):
    gw = HEADS_PER_GROUP * HEAD_DIM
    _scan_tables(dta_ref, hm_s, pre_s, nc)

    def conv_step(c, carry):
        _conv_silu_chunk(xs_ref, cwx_ref, cbx_ref, xs_s, buf_s, c, nc, gw)
        _conv_silu_chunk(b_ref, cwb_ref, cbb_ref, b_s, buf_s, c, nc, D_STATE)
        _conv_silu_chunk(c_ref, cwc_ref, cbc_ref, c_s, buf_s, c, nc, D_STATE)
        return carry

    lax.fori_loop(0, nc, conv_step, 0)

    refs = (pre_s, xs_s, b_s, c_s, h_s)

    h_s[...] = jnp.zeros_like(h_s)

    def fwd_step(c, carry):
        r0 = pl.multiple_of(c * CHUNK, CHUNK)
        y = _ssd_chunk(c, False, refs)
        yacc_s[pl.ds(r0, CHUNK), :] = y + dsk_ref[...] * xs_s[pl.ds(r0, CHUNK), :].astype(F32)
        return carry

    lax.fori_loop(0, nc, fwd_step, 0)

    h_s[...] = jnp.zeros_like(h_s)

    def bwd_step(t, carry):
        c = nc - 1 - t
        r0 = pl.multiple_of(c * CHUNK, CHUNK)
        y = _ssd_chunk(c, True, refs)
        y_ref[0, pl.ds(r0, CHUNK), :] = (yacc_s[pl.ds(r0, CHUNK), :] + y).astype(BF16)
        return carry

    lax.fori_loop(0, nc, bwd_step, 0)


def _ssd(p3, dt3, conv_w, conv_b, dskip, layer):
    Bsz, L, _ = p3.shape
    nc = L // CHUNK
    gw = HEADS_PER_GROUP * HEAD_DIM
    xs_blk, b_blk, c_blk = P_XS // gw, P_B // D_STATE, P_C // D_STATE
    cw_b0, cw_c0 = D_INNER // D_STATE, (D_INNER + GN) // D_STATE
    return pl.pallas_call(
        functools.partial(_ssd_body, nc=nc),
        grid=(Bsz, N_GROUPS),
        in_specs=[
            pl.BlockSpec((1, L, gw), lambda b, g: (b, 0, xs_blk + g)),
            pl.BlockSpec((1, L, D_STATE), lambda b, g: (b, 0, b_blk + g)),
            pl.BlockSpec((1, L, D_STATE), lambda b, g: (b, 0, c_blk + g)),
            pl.BlockSpec((1, L, DT_PAD), lambda b, g: (b, 0, 0)),
            pl.BlockSpec((None, CONV_W, gw), lambda b, g: (layer, 0, g)),
            pl.BlockSpec((None, CONV_W, D_STATE), lambda b, g: (layer, 0, cw_b0 + g)),
            pl.BlockSpec((None, CONV_W, D_STATE), lambda b, g: (layer, 0, cw_c0 + g)),
            pl.BlockSpec((None, 1, gw), lambda b, g: (layer, 0, g)),
            pl.BlockSpec((None, 1, D_STATE), lambda b, g: (layer, 0, cw_b0 + g)),
            pl.BlockSpec((None, 1, D_STATE), lambda b, g: (layer, 0, cw_c0 + g)),
            pl.BlockSpec((None, 1, gw), lambda b, g: (layer, 0, g)),
        ],
        out_specs=pl.BlockSpec((1, L, gw), lambda b, g: (b, 0, g)),
        out_shape=jax.ShapeDtypeStruct((Bsz, L, D_INNER), BF16),
        scratch_shapes=[
            pltpu.VMEM((L, gw), BF16),
            pltpu.VMEM((L, D_STATE), BF16),
            pltpu.VMEM((L, D_STATE), BF16),
            pltpu.VMEM((CHUNK + 2 * HALO, gw), F32),
            pltpu.VMEM((DT_PAD, L), F32),
            pltpu.VMEM((2 * ROWS_PER_DIR, L), F32),
            pltpu.VMEM((L, gw), F32),
            pltpu.VMEM((D_STATE, gw), F32),
        ],
        compiler_params=pltpu.CompilerParams(
            dimension_semantics=("parallel", "parallel"),
            vmem_limit_bytes=VMEM_LIMIT_BYTES),
        name="ssd",
    )(p3, p3, p3, dt3, conv_w, conv_w, conv_w, conv_b, conv_b, conv_b, dskip)


KV_ROWS = 3 * WINDOW
KV_DUP = 2 * KV_DIM
assert A_DIM ** -0.5 == 0.125


def _attn_body(sink_ref, q_ref, k_ref, v_ref, o_ref, bias_s, s_s, p_s, vt_s, *, layer, nb):
    i = pl.program_id(1)
    start = pl.multiple_of(jnp.clip(i - 1, 0, nb - 3) * WINDOW, WINDOW)
    heads_per_kv = A_HEADS // A_KV
    grp_lanes = heads_per_kv * WINDOW

    @pl.when((i <= 1) | (i == nb - 1))
    def _():
        off = i * WINDOW - start
        key = lax.broadcasted_iota(jnp.int32, (KV_ROWS, WINDOW), 0)
        qry = lax.broadcasted_iota(jnp.int32, (KV_ROWS, WINDOW), 1)
        dist = jnp.abs(off + qry - key)
        valid = dist <= WINDOW
        distf = dist.astype(F32)
        for h in range(A_HEADS):
            slope = 2.0 ** (-8.0 * (h + 1) / A_HEADS)
            bias_s[h // heads_per_kv, :, (h % heads_per_kv) * WINDOW:(h % heads_per_kv + 1) * WINDOW] = (
                jnp.where(valid, -slope * distf, NEG))

    low_q = lax.broadcasted_iota(jnp.int32, (WINDOW, LANES), 1) < A_DIM
    col_head = lax.broadcasted_iota(jnp.int32, (1, grp_lanes), 1) // WINDOW
    top = lax.broadcasted_iota(jnp.int32, (LANES, KV_ROWS), 0) < A_DIM
    trans_b = (((1,), (1,)), ((), ()))

    for g in range(A_KV):
        parts = []
        for pr in range(2 * g, 2 * g + 2):
            qp = q_ref[0, :, pr * LANES:(pr + 1) * LANES] * 0.125
            zero = jnp.zeros_like(qp)
            parts += [jnp.where(low_q, qp, zero), jnp.where(low_q, zero, qp)]
        kd = k_ref[0, pl.ds(start, KV_ROWS), g * LANES:(g + 1) * LANES]
        s_s[g] = lax.dot_general(kd, jnp.concatenate(parts, axis=0), trans_b, preferred_element_type=F32)
        vt = v_ref[0, pl.ds(start, KV_ROWS), g * LANES:(g + 1) * LANES].astype(F32).T
        vt_s[g] = jnp.where(top, vt, 1.0).astype(BF16)

    maxima, sinks = [], []
    for g in range(A_KV):
        sink = jnp.zeros((1, grp_lanes), F32)
        for e in range(heads_per_kv):
            sink = jnp.where(col_head == e, sink_ref[layer, g * heads_per_kv + e], sink)
        s = s_s[g] + bias_s[g]
        m = jnp.maximum(jnp.max(s, axis=0, keepdims=True), sink)
        p_s[g] = jnp.exp(s - m).astype(BF16)
        maxima.append(m)
        sinks.append(sink)

    for g in range(A_KV):
        res = jnp.dot(vt_s[g], p_s[g], preferred_element_type=F32)
        o_t = res[0:A_DIM] / (res[A_DIM:2 * A_DIM] + jnp.exp(sinks[g] - maxima[g]))
        o_grp = jnp.concatenate([o_t[:, e * WINDOW:(e + 1) * WINDOW] for e in range(heads_per_kv)], axis=0)
        o_ref[0, :, g * 2 * LANES:(g + 1) * 2 * LANES] = o_grp.T.astype(BF16)


def _attn(p3, sink, layer):
    Bsz, L, _ = p3.shape
    nb = L // WINDOW
    return pl.pallas_call(
        functools.partial(_attn_body, layer=layer, nb=nb),
        grid=(Bsz, nb),
        in_specs=[
            pl.BlockSpec(memory_space=pltpu.SMEM),
            pl.BlockSpec((1, WINDOW, Q_DIM), lambda b, i: (b, i, P_Q // Q_DIM)),
            pl.BlockSpec((1, L, KV_DUP), lambda b, i: (b, 0, P_K // KV_DUP)),
            pl.BlockSpec((1, L, KV_DUP), lambda b, i: (b, 0, P_V // KV_DUP)),
        ],
        out_specs=pl.BlockSpec((1, WINDOW, Q_DIM), lambda b, i: (b, i, 0)),
        out_shape=jax.ShapeDtypeStruct((Bsz, L, Q_DIM), BF16),
        scratch_shapes=[
            pltpu.VMEM((A_KV, KV_ROWS, 4 * WINDOW), F32),
            pltpu.VMEM((A_KV, KV_ROWS, 4 * WINDOW), F32),
            pltpu.VMEM((A_KV, KV_ROWS, 4 * WINDOW), BF16),
            pltpu.VMEM((A_KV, LANES, KV_ROWS), BF16),
        ],
        compiler_params=pltpu.CompilerParams(
            dimension_semantics=("parallel", "arbitrary"),
            vmem_limit_bytes=VMEM_LIMIT_BYTES),
        name="attn",
    )(sink, p3, p3, p3)


def _mix_body(y_ref, z_ref, at_ref, ga_ref, gb_ref, x_ref, nssd_ref, npost_ref,
              wssd_ref, wat_ref, wout_ref, o_ref):
    u = y_ref[...].astype(F32) * jax.nn.silu(z_ref[...].astype(F32))
    un = _rms(u, nssd_ref[...]).astype(BF16)
    y_ssd = jnp.dot(un, wssd_ref[...], preferred_element_type=F32)
    y_at = jnp.dot(at_ref[...], wat_ref[...], preferred_element_type=F32)
    mix_in = (jax.nn.sigmoid(ga_ref[...].astype(F32)) * y_ssd
              + jax.nn.sigmoid(gb_ref[...].astype(F32)) * y_at).astype(BF16)
    mix = jnp.dot(mix_in, wout_ref[...], preferred_element_type=F32)
    o_ref[...] = x_ref[...] + _rms(mix, npost_ref[...])


def _mix(y2, p2, at2, x2, n_ssd, n_post, w_ssd, w_at, w_out, layer, tm):
    T = x2.shape[0]
    row = lambda blk: (lambda i: (i, blk))
    par = lambda i: (layer, 0, 0)
    return pl.pallas_call(
        _mix_body,
        grid=(T // tm,),
        in_specs=[
            pl.BlockSpec((tm, D_INNER), row(0)),
            pl.BlockSpec((tm, D_INNER), row(P_Z // D_INNER)),
            pl.BlockSpec((tm, Q_DIM), row(0)),
            pl.BlockSpec((tm, D_MODEL), row(P_GA // D_MODEL)),
            pl.BlockSpec((tm, D_MODEL), row(P_GB // D_MODEL)),
            pl.BlockSpec((tm, D_MODEL), row(0)),
            pl.BlockSpec((None, 1, D_INNER), par),
            pl.BlockSpec((None, 1, D_MODEL), par),
            pl.BlockSpec((None, D_INNER, D_MODEL), par),
            pl.BlockSpec((None, Q_DIM, D_MODEL), par),
            pl.BlockSpec((None, D_MODEL, D_MODEL), par),
        ],
        out_specs=pl.BlockSpec((tm, D_MODEL), row(0)),
        out_shape=jax.ShapeDtypeStruct((T, D_MODEL), F32),
        compiler_params=pltpu.CompilerParams(
            dimension_semantics=("parallel",),
            vmem_limit_bytes=VMEM_LIMIT_BYTES),
        name="mix",
    )(y2, p2, at2, p2, p2, x2, n_ssd, n_post, w_ssd, w_at, w_out)


FF_CHUNK = D_FF // 2


def _ffn_body(x_ref, npre_ref, npost_ref, win_ref, wout_ref, o_ref):
    x = x_ref[...]
    h = _rms(x, npre_ref[...]).astype(BF16)
    acc = jnp.zeros(x.shape, F32)
    for c0 in range(0, D_FF, FF_CHUNK):
        gate = jnp.dot(h, win_ref[:, c0:c0 + FF_CHUNK], preferred_element_type=F32)
        up = jnp.dot(h, win_ref[:, D_FF + c0:D_FF + c0 + FF_CHUNK], preferred_element_type=F32)
        act = (jax.nn.silu(gate) * up).astype(BF16)
        acc = acc + jnp.dot(act, wout_ref[c0:c0 + FF_CHUNK, :], preferred_element_type=F32)
    o_ref[...] = x + _rms(acc, npost_ref[...])


def _ffn(x2, n_pre, n_post, w_in, w_out, layer, tm):
    T = x2.shape[0]
    par = lambda i: (layer, 0, 0)
    return pl.pallas_call(
        _ffn_body,
        grid=(T // tm,),
        in_specs=[
            pl.BlockSpec((tm, D_MODEL), lambda i: (i, 0)),
            pl.BlockSpec((None, 1, D_MODEL), par),
            pl.BlockSpec((None, 1, D_MODEL), par),
            pl.BlockSpec((None, D_MODEL, 2 * D_FF), par),
            pl.BlockSpec((None, D_FF, D_MODEL), par),
        ],
        out_specs=pl.BlockSpec((tm, D_MODEL), lambda i: (i, 0)),
        out_shape=jax.ShapeDtypeStruct((T, D_MODEL), F32),
        compiler_params=pltpu.CompilerParams(
            dimension_semantics=("parallel",),
            vmem_limit_bytes=VMEM_LIMIT_BYTES),
        name="ffn",
    )(x2, n_pre, n_post, w_in, w_out)


def _tile(total, want):
    t = min(total, want)
    assert total % t == 0, (total, want)
    return t


def kernel(x, pre_mix_norm, w_in, conv_w, conv_b, dt_bias_f, dt_bias_b, A_log_f, A_log_b, D_skip,
           ssd_norm, w_ssd_proj, attn_sink, w_attn_proj, w_out, post_mix_norm, pre_ffn_norm,
           w_ffn_in, w_ffn_out, post_ffn_norm):
    Bsz, L, D = x.shape
    depth = w_in.shape[0]
    assert D == D_MODEL and L % CHUNK == 0 and L >= 3 * WINDOW
    T = Bsz * L

    sizes = dict(z=D_INNER, xs=D_INNER, b=GN, c=GN, dt=2 * N_HEADS, q=Q_DIM, k=KV_DIM, v=KV_DIM,
                 ga=D_MODEL, gb=D_MODEL)
    cols, start = {}, 0
    for name, size in sizes.items():
        cols[name] = w_in[:, :, start:start + size]
        start += size
    assert start == w_in.shape[-1]
    for name in ("k", "v"):
        per_head = cols[name].reshape(depth, D, A_KV, 1, A_DIM)
        cols[name] = jnp.broadcast_to(per_head, (depth, D, A_KV, 2, A_DIM)).reshape(depth, D, 2 * KV_DIM)
    w_main = jnp.concatenate(
        [cols[n] for n in ("z", "xs", "q", "ga", "gb", "b", "c", "k", "v")], axis=-1).astype(BF16)
    assert w_main.shape[-1] == NP
    w_dt = jnp.concatenate([cols["dt"], cols["dt"]], axis=-1).astype(BF16)
    assert w_dt.shape[-1] == DT_PAD
    bias = jnp.concatenate([dt_bias_f, dt_bias_b], axis=-1)
    head_params = jnp.stack(
        [jnp.concatenate([bias, bias], axis=-1),
         jnp.concatenate([jnp.zeros_like(bias), A_log_f, A_log_b], axis=-1)]
        + [jnp.zeros((depth, LANES), F32)] * 6, axis=1)
    dskip = jnp.repeat(D_skip, HEAD_DIM, axis=-1)[:, None, :]
    vec = lambda a: a[:, None, :]
    conv_b3 = vec(conv_b)
    w_ssd_b, w_at_b, w_out_b = (w.astype(BF16) for w in (w_ssd_proj, w_attn_proj, w_out))
    w_fin_b, w_fout_b = w_ffn_in.astype(BF16), w_ffn_out.astype(BF16)

    tm_in = _tile(T, 2048)
    tm_mix = _tile(T, 512)
    tm_ffn = _tile(T, 512)

    x2 = x.reshape(T, D)
    for i in range(depth):
        p2, dt2 = _in_proj(x2, vec(pre_mix_norm), w_main, w_dt, head_params, i, tm_in, 1024)
        p3 = p2.reshape(Bsz, L, NP)
        y3 = _ssd(p3, dt2.reshape(Bsz, L, DT_PAD), conv_w, conv_b3, dskip, i)
        at3 = _attn(p3, attn_sink, i)
        x2 = _mix(y3.reshape(T, D_INNER), p2, at3.reshape(T, Q_DIM), x2, vec(ssd_norm),
                  vec(post_mix_norm), w_ssd_b, w_at_b, w_out_b, i, tm_mix)
        x2 = _ffn(x2, vec(pre_ffn_norm), vec(post_ffn_norm), w_fin_b, w_fout_b, i, tm_ffn)
    return x2.reshape(Bsz, L, D)
```

```python
import functools

import jax
import jax.numpy as jnp
from jax import lax
from jax.experimental import pallas as pl
from jax.experimental.pallas import tpu as pltpu

F32 = jnp.float32
BF16 = jnp.bfloat16
EPS = 1e-6
NEG = -1e30

D_MODEL = 1024
D_INNER = 2048
HEAD_DIM = 64
N_HEADS = 32
N_GROUPS = 4
HEADS_PER_GROUP = N_HEADS // N_GROUPS
D_STATE = 128
GN = N_GROUPS * D_STATE
CONV_DIM = D_INNER + 2 * GN
CONV_W = 5
CHUNK = 128
A_HEADS = 16
A_KV = 4
A_DIM = 64
Q_DIM = A_HEADS * A_DIM
KV_DIM = A_KV * A_DIM
WINDOW = 128
D_FF = 2816

P_Z = 0
P_XS = P_Z + D_INNER
P_Q = P_XS + D_INNER
P_GA = P_Q + Q_DIM
P_GB = P_GA + D_MODEL
P_B = P_GB + D_MODEL
P_C = P_B + GN
P_K = P_C + GN
P_V = P_K + 2 * KV_DIM
NP = P_V + 2 * KV_DIM
DT_PAD = 128

VMEM_LIMIT_BYTES = 56 * 1024 * 1024
LANES = 128


def _rms(x, w):
    ms = jnp.mean(x * x, axis=-1, keepdims=True)
    return x * lax.rsqrt(ms + EPS) * w


def _softplus(x):
    return jnp.maximum(x, 0.0) + jnp.log1p(jnp.exp(-jnp.abs(x)))


def _inproj_body(x_ref, nw_ref, w_ref, wdt_ref, hp_ref, p_ref, dt_ref, h_scr):
    @pl.when(pl.program_id(1) == 0)
    def _():
        h = _rms(x_ref[...], nw_ref[...]).astype(BF16)
        h_scr[...] = h
        raw = jnp.dot(h, wdt_ref[...], preferred_element_type=F32)
        dt = _softplus(raw + hp_ref[0:1, :])
        lane = lax.broadcasted_iota(jnp.int32, (1, DT_PAD), 1)
        dt_ref[...] = (dt * jnp.where(lane < 2 * N_HEADS, 1.0, -jnp.exp(hp_ref[1:2, :]))).T

    p_ref[...] = jnp.dot(h_scr[...], w_ref[...], preferred_element_type=F32).astype(BF16)


def _in_proj(x2, norm_w, w_main, w_dt, head_params, layer, tm, tn):
    T = x2.shape[0]
    return pl.pallas_call(
        _inproj_body,
        grid=(T // tm, NP // tn),
        in_specs=[
            pl.BlockSpec((tm, D_MODEL), lambda i, j: (i, 0)),
            pl.BlockSpec((None, 1, D_MODEL), lambda i, j: (layer, 0, 0)),
            pl.BlockSpec((None, D_MODEL, tn), lambda i, j: (layer, 0, j)),
            pl.BlockSpec((None, D_MODEL, DT_PAD), lambda i, j: (layer, 0, 0)),
            pl.BlockSpec((None, 8, LANES), lambda i, j: (layer, 0, 0)),
        ],
        out_specs=[
            pl.BlockSpec((tm, tn), lambda i, j: (i, j)),
            pl.BlockSpec((DT_PAD, tm), lambda i, j: (0, i)),
        ],
        out_shape=[
            jax.ShapeDtypeStruct((T, NP), BF16),
            jax.ShapeDtypeStruct((DT_PAD, T), F32),
        ],
        scratch_shapes=[pltpu.VMEM((tm, D_MODEL), BF16)],
        compiler_params=pltpu.CompilerParams(
            dimension_semantics=("parallel", "arbitrary"),
            vmem_limit_bytes=VMEM_LIMIT_BYTES),
        name="in_proj",
    )(x2, norm_w, w_main, w_dt, head_params)


HALO = 16
CONV_WIN = CHUNK + 2 * HALO
SIDE_TAPS = tuple(k for k in range(CONV_W) if k != CONV_W // 2)


def _shift_matrix():
    row = lax.broadcasted_iota(jnp.int32, (len(SIDE_TAPS) * CHUNK, CONV_WIN), 0)
    col = lax.broadcasted_iota(jnp.int32, (len(SIDE_TAPS) * CHUNK, CONV_WIN), 1)
    blk = row // CHUNK
    tap = jnp.where(blk < CONV_W // 2, blk, blk + 1)
    return (col == (row - blk * CHUNK) + HALO - CONV_W // 2 + tap).astype(F32).astype(BF16)


def _conv_silu_chunk(streams, shift_ref, c, nc):
    r0 = pl.multiple_of(c * CHUNK, CHUNK)
    prev0 = pl.multiple_of(jnp.maximum(r0 - HALO, 0), HALO)
    next0 = pl.multiple_of(jnp.minimum(r0 + CHUNK, (nc - 1) * CHUNK + CHUNK - HALO), HALO)

    def rows(start, n):
        return jnp.concatenate([src_ref[0, pl.ds(start, n), :] for src_ref, _, _, _ in streams], axis=1)

    prev, main, nxt = rows(prev0, HALO), rows(r0, CHUNK), rows(next0, HALO)
    prev = jnp.where(c > 0, prev, jnp.zeros_like(prev))
    nxt = jnp.where(c < nc - 1, nxt, jnp.zeros_like(nxt))
    win = jnp.concatenate([prev, main, nxt], axis=0)
    shifted = jnp.dot(shift_ref[...], win, preferred_element_type=F32)
    col = 0
    for _, w_ref, b_ref, dst_ref in streams:
        width = dst_ref.shape[-1]
        acc = b_ref[...] + main[:, col:col + width].astype(F32) * w_ref[CONV_W // 2:CONV_W // 2 + 1, :]
        for i, k in enumerate(SIDE_TAPS):
            acc = acc + shifted[i * CHUNK:(i + 1) * CHUNK, col:col + width] * w_ref[k:k + 1, :]
        dst_ref[pl.ds(r0, CHUNK), :] = (acc * jax.nn.sigmoid(acc)).astype(BF16)
        col += width


def _lane_cumsum(x, reverse):
    n = x.shape[1]
    lane = lax.broadcasted_iota(jnp.int32, x.shape, 1) & (LANES - 1)
    k = 1
    while k < LANES:
        if reverse:
            sh = pltpu.roll(x, n - k, axis=1)
            x = x + jnp.where(lane < LANES - k, sh, 0.0)
        else:
            sh = pltpu.roll(x, k, axis=1)
            x = x + jnp.where(lane >= k, sh, 0.0)
        k *= 2
    return x


ROW_CUM, ROW_SRC, ROW_W, ROW_CD, ROWS_PER_DIR = 0, 8, 16, 24, 32


def _scan_tables(dta_ref, pre_s):
    log2e = 1.4426950408889634
    g8 = pl.program_id(1) * HEADS_PER_GROUP
    for d in range(2):
        dt = dta_ref[pl.ds(pl.multiple_of(d * N_HEADS + g8, 8), 8), :]
        a = dta_ref[pl.ds(pl.multiple_of((2 + d) * N_HEADS + g8, 8), 8), :]
        inc = _lane_cumsum(a, False)
        rev = _lane_cumsum(a, True)
        cum, other = (inc, rev) if d == 0 else (rev, inc)
        base = d * ROWS_PER_DIR
        pre_s[base + ROW_CUM:base + ROW_CUM + 8, :] = cum * log2e
        pre_s[base + ROW_SRC:base + ROW_SRC + 8, :] = (cum - jnp.log(dt)) * log2e
        pre_s[base + ROW_W:base + ROW_W + 8, :] = dt * jnp.exp(other - a)
        pre_s[base + ROW_CD:base + ROW_CD + 8, :] = jnp.exp(inc + rev - a)


def _stage_operands(c, reverse, slot, refs):
    (pre_s, xs_s, b_s, c_s, h_s, lhs_s, btw_s) = refs
    r0 = pl.multiple_of(c * CHUNK, CHUNK)
    base = ROWS_PER_DIR if reverse else 0
    cum8 = pre_s[base + ROW_CUM:base + ROW_CUM + 8, pl.ds(r0, CHUNK)]
    src8 = pre_s[base + ROW_SRC:base + ROW_SRC + 8, pl.ds(r0, CHUNK)]
    w8 = pre_s[base + ROW_W:base + ROW_W + 8, pl.ds(r0, CHUNK)]
    bc = b_s[pl.ds(r0, CHUNK), :]
    cc = c_s[pl.ds(r0, CHUNK), :]
    cb = lax.dot_general(cc, bc, (((1,), (1,)), ((), ())), preferred_element_type=F32)
    cf = cc.astype(F32)
    bT = bc.astype(F32).T
    li = lax.broadcasted_iota(jnp.int32, (CHUNK, CHUNK), 0)
    si = lax.broadcasted_iota(jnp.int32, (CHUNK, CHUNK), 1)
    mask = (si >= li) if reverse else (si <= li)
    for j in range(HEADS_PER_GROUP):
        acol = jnp.broadcast_to(cum8[j:j + 1, :], (CHUNK, CHUNK)).T
        m = cb * jnp.exp2(jnp.where(mask, acol - src8[j:j + 1, :], NEG))
        lhs_s[slot, j, :, 0:CHUNK] = m.astype(BF16)
        lhs_s[slot, j, :, CHUNK:CHUNK + D_STATE] = (cf * jnp.exp2(acol)).astype(BF16)
        btw_s[slot, j] = (bT * w8[j:j + 1, :]).astype(BF16)


def _apply_chunk(c, reverse, slot, refs):
    (pre_s, xs_s, b_s, c_s, h_s, lhs_s, btw_s) = refs
    r0 = pl.multiple_of(c * CHUNK, CHUNK)
    base = ROWS_PER_DIR if reverse else 0
    cd8 = pre_s[base + ROW_CD:base + ROW_CD + 8, pl.ds(r0, CHUNK)]
    xs = xs_s[pl.ds(r0, CHUNK), :]
    low = lax.broadcasted_iota(jnp.int32, (CHUNK, LANES), 1) < HEAD_DIM
    y_parts = []
    for p in range(HEADS_PER_GROUP // 2):
        xp = xs[:, p * LANES:(p + 1) * LANES]
        hp_state = h_s[:, p * LANES:(p + 1) * LANES]
        rhs = jnp.concatenate([xp, hp_state.astype(BF16)], axis=0)
        res = [jnp.dot(lhs_s[slot, 2 * p + e], rhs, preferred_element_type=F32) for e in range(2)]
        st = [jnp.dot(btw_s[slot, 2 * p + e], xp, preferred_element_type=F32) for e in range(2)]
        y_parts.append(jnp.where(low, res[0], res[1]))
        cd_pair = jnp.where(low[0:1, :], cd8[2 * p:2 * p + 1, :], cd8[2 * p + 1:2 * p + 2, :])
        h_s[:, p * LANES:(p + 1) * LANES] = hp_state * cd_pair + jnp.where(low, st[0], st[1])
    return jnp.concatenate(y_parts, axis=1)


def _ssd_body(xs_ref, b_ref, c_ref, dta_ref, cwx_ref, cwb_ref, cwc_ref, cbx_ref, cbb_ref, cbc_ref,
              dsk_ref, y_ref,
              xs_s, b_s, c_s, shift_s, pre_s, yacc_s, h_s, lhs_s, btw_s, *, nc):
    _scan_tables(dta_ref, pre_s)
    shift_s[...] = _shift_matrix()

    def conv_step(c, carry):
        _conv_silu_chunk([(xs_ref, cwx_ref, cbx_ref, xs_s)], shift_s, c, nc)
        _conv_silu_chunk([(b_ref, cwb_ref, cbb_ref, b_s), (c_ref, cwc_ref, cbc_ref, c_s)], shift_s, c, nc)
        return carry

    lax.fori_loop(0, nc, conv_step, 0, unroll=2)

    refs = (pre_s, xs_s, b_s, c_s, h_s, lhs_s, btw_s)

    def scan(reverse, emit):
        chunk = (lambda k: nc - 1 - k) if reverse else (lambda k: k)
        h_s[...] = jnp.zeros_like(h_s)
        _stage_operands(chunk(0), reverse, 0, refs)

        def step(t, carry):
            k = 2 * t
            _stage_operands(chunk(k + 1), reverse, 1, refs)
            emit(chunk(k), _apply_chunk(chunk(k), reverse, 0, refs))
            _stage_operands(chunk(jnp.minimum(k + 2, nc - 1)), reverse, 0, refs)
            emit(chunk(k + 1), _apply_chunk(chunk(k + 1), reverse, 1, refs))
            return carry

        lax.fori_loop(0, nc // 2, step, 0)

    def emit_fwd(c, y):
        r0 = pl.multiple_of(c * CHUNK, CHUNK)
        yacc_s[pl.ds(r0, CHUNK), :] = y + dsk_ref[...] * xs_s[pl.ds(r0, CHUNK), :].astype(F32)

    def emit_bwd(c, y):
        r0 = pl.multiple_of(c * CHUNK, CHUNK)
        y_ref[0, pl.ds(r0, CHUNK), :] = (yacc_s[pl.ds(r0, CHUNK), :] + y).astype(BF16)

    scan(False, emit_fwd)
    scan(True, emit_bwd)


def _ssd(p3, dta, conv_w, conv_b, dskip, layer):
    Bsz, L, _ = p3.shape
    nc = L // CHUNK
    gw = HEADS_PER_GROUP * HEAD_DIM
    xs_blk, b_blk, c_blk = P_XS // gw, P_B // D_STATE, P_C // D_STATE
    cw_b0, cw_c0 = D_INNER // D_STATE, (D_INNER + GN) // D_STATE
    return pl.pallas_call(
        functools.partial(_ssd_body, nc=nc),
        grid=(Bsz, N_GROUPS),
        in_specs=[
            pl.BlockSpec((1, L, gw), lambda b, g: (b, 0, xs_blk + g)),
            pl.BlockSpec((1, L, D_STATE), lambda b, g: (b, 0, b_blk + g)),
            pl.BlockSpec((1, L, D_STATE), lambda b, g: (b, 0, c_blk + g)),
            pl.BlockSpec((DT_PAD, L), lambda b, g: (0, b)),
            pl.BlockSpec((None, CONV_W, gw), lambda b, g: (layer, 0, g)),
            pl.BlockSpec((None, CONV_W, D_STATE), lambda b, g: (layer, 0, cw_b0 + g)),
            pl.BlockSpec((None, CONV_W, D_STATE), lambda b, g: (layer, 0, cw_c0 + g)),
            pl.BlockSpec((None, 1, gw), lambda b, g: (layer, 0, g)),
            pl.BlockSpec((None, 1, D_STATE), lambda b, g: (layer, 0, cw_b0 + g)),
            pl.BlockSpec((None, 1, D_STATE), lambda b, g: (layer, 0, cw_c0 + g)),
            pl.BlockSpec((None, 1, gw), lambda b, g: (layer, 0, g)),
        ],
        out_specs=pl.BlockSpec((1, L, gw), lambda b, g: (b, 0, g)),
        out_shape=jax.ShapeDtypeStruct((Bsz, L, D_INNER), BF16),
        scratch_shapes=[
            pltpu.VMEM((L, gw), BF16),
            pltpu.VMEM((L, D_STATE), BF16),
            pltpu.VMEM((L, D_STATE), BF16),
            pltpu.VMEM((len(SIDE_TAPS) * CHUNK, CONV_WIN), BF16),
            pltpu.VMEM((2 * ROWS_PER_DIR, L), F32),
            pltpu.VMEM((L, gw), F32),
            pltpu.VMEM((D_STATE, gw), F32),
            pltpu.VMEM((2, HEADS_PER_GROUP, CHUNK, CHUNK + D_STATE), BF16),
            pltpu.VMEM((2, HEADS_PER_GROUP, D_STATE, CHUNK), BF16),
        ],
        compiler_params=pltpu.CompilerParams(
            dimension_semantics=("parallel", "parallel"),
            vmem_limit_bytes=VMEM_LIMIT_BYTES),
        name="ssd",
    )(p3, p3, p3, dta, conv_w, conv_w, conv_w, conv_b, conv_b, conv_b, dskip)


KV_ROWS = 3 * WINDOW
KV_DUP = 2 * KV_DIM
assert A_DIM ** -0.5 == 0.125


def _attn_body(sink_ref, q_ref, k_ref, v_ref, o_ref, bias_s, s_s, p_s, vt_s, *, layer, nb):
    i = pl.program_id(1)
    start = pl.multiple_of(jnp.clip(i - 1, 0, nb - 3) * WINDOW, WINDOW)
    heads_per_kv = A_HEADS // A_KV
    grp_lanes = heads_per_kv * WINDOW

    @pl.when((i <= 1) | (i == nb - 1))
    def _():
        off = i * WINDOW - start
        key = lax.broadcasted_iota(jnp.int32, (KV_ROWS, WINDOW), 0)
        qry = lax.broadcasted_iota(jnp.int32, (KV_ROWS, WINDOW), 1)
        dist = jnp.abs(off + qry - key)
        valid = dist <= WINDOW
        distf = dist.astype(F32)
        for h in range(A_HEADS):
            slope = 2.0 ** (-8.0 * (h + 1) / A_HEADS)
            bias_s[h // heads_per_kv, :, (h % heads_per_kv) * WINDOW:(h % heads_per_kv + 1) * WINDOW] = (
                jnp.where(valid, -slope * distf, NEG))

    low_q = lax.broadcasted_iota(jnp.int32, (WINDOW, LANES), 1) < A_DIM
    col_head = lax.broadcasted_iota(jnp.int32, (1, grp_lanes), 1) // WINDOW
    top = lax.broadcasted_iota(jnp.int32, (LANES, KV_ROWS), 0) < A_DIM
    trans_b = (((1,), (1,)), ((), ()))

    for g in range(A_KV):
        parts = []
        for pr in range(2 * g, 2 * g + 2):
            qp = q_ref[0, :, pr * LANES:(pr + 1) * LANES] * 0.125
            zero = jnp.zeros_like(qp)
            parts += [jnp.where(low_q, qp, zero), jnp.where(low_q, zero, qp)]
        kd = k_ref[0, pl.ds(start, KV_ROWS), g * LANES:(g + 1) * LANES]
        s_s[g] = lax.dot_general(kd, jnp.concatenate(parts, axis=0), trans_b, preferred_element_type=F32)
        vt = v_ref[0, pl.ds(start, KV_ROWS), g * LANES:(g + 1) * LANES].astype(F32).T
        vt_s[g] = jnp.where(top, vt, 1.0).astype(BF16)

    maxima, sinks = [], []
    for g in range(A_KV):
        sink = jnp.zeros((1, grp_lanes), F32)
        for e in range(heads_per_kv):
            sink = jnp.where(col_head == e, sink_ref[layer, g * heads_per_kv + e], sink)
        s = s_s[g] + bias_s[g]
        m = jnp.maximum(jnp.max(s, axis=0, keepdims=True), sink)
        p_s[g] = jnp.exp(s - m).astype(BF16)
        maxima.append(m)
        sinks.append(sink)

    for g in range(A_KV):
        res = jnp.dot(vt_s[g], p_s[g], preferred_element_type=F32)
        o_t = res[0:A_DIM] / (res[A_DIM:2 * A_DIM] + jnp.exp(sinks[g] - maxima[g]))
        o_grp = jnp.concatenate([o_t[:, e * WINDOW:(e + 1) * WINDOW] for e in range(heads_per_kv)], axis=0)
        o_ref[0, :, g * 2 * LANES:(g + 1) * 2 * LANES] = o_grp.T.astype(BF16)


def _attn(p3, sink, layer):
    Bsz, L, _ = p3.shape
    nb = L // WINDOW
    return pl.pallas_call(
        functools.partial(_attn_body, layer=layer, nb=nb),
        grid=(Bsz, nb),
        in_specs=[
            pl.BlockSpec(memory_space=pltpu.SMEM),
            pl.BlockSpec((1, WINDOW, Q_DIM), lambda b, i: (b, i, P_Q // Q_DIM)),
            pl.BlockSpec((1, L, KV_DUP), lambda b, i: (b, 0, P_K // KV_DUP)),
            pl.BlockSpec((1, L, KV_DUP), lambda b, i: (b, 0, P_V // KV_DUP)),
        ],
        out_specs=pl.BlockSpec((1, WINDOW, Q_DIM), lambda b, i: (b, i, 0)),
        out_shape=jax.ShapeDtypeStruct((Bsz, L, Q_DIM), BF16),
        scratch_shapes=[
            pltpu.VMEM((A_KV, KV_ROWS, 4 * WINDOW), F32),
            pltpu.VMEM((A_KV, KV_ROWS, 4 * WINDOW), F32),
            pltpu.VMEM((A_KV, KV_ROWS, 4 * WINDOW), BF16),
            pltpu.VMEM((A_KV, LANES, KV_ROWS), BF16),
        ],
        compiler_params=pltpu.CompilerParams(
            dimension_semantics=("parallel", "arbitrary"),
            vmem_limit_bytes=VMEM_LIMIT_BYTES),
        name="attn",
    )(sink, p3, p3, p3)


def _mix_body(y_ref, z_ref, at_ref, ga_ref, gb_ref, x_ref, nssd_ref, npost_ref,
              wssd_ref, wat_ref, wout_ref, o_ref):
    u = y_ref[...].astype(F32) * jax.nn.silu(z_ref[...].astype(F32))
    un = _rms(u, nssd_ref[...]).astype(BF16)
    y_ssd = jnp.dot(un, wssd_ref[...], preferred_element_type=F32)
    y_at = jnp.dot(at_ref[...], wat_ref[...], preferred_element_type=F32)
    mix_in = (jax.nn.sigmoid(ga_ref[...].astype(F32)) * y_ssd
              + jax.nn.sigmoid(gb_ref[...].astype(F32)) * y_at).astype(BF16)
    mix = jnp.dot(mix_in, wout_ref[...], preferred_element_type=F32)
    o_ref[...] = x_ref[...] + _rms(mix, npost_ref[...])


def _mix(y2, p2, at2, x2, n_ssd, n_post, w_ssd, w_at, w_out, layer, tm):
    T = x2.shape[0]
    row = lambda blk: (lambda i: (i, blk))
    par = lambda i: (layer, 0, 0)
    return pl.pallas_call(
        _mix_body,
        grid=(T // tm,),
        in_specs=[
            pl.BlockSpec((tm, D_INNER), row(0)),
            pl.BlockSpec((tm, D_INNER), row(P_Z // D_INNER)),
            pl.BlockSpec((tm, Q_DIM), row(0)),
            pl.BlockSpec((tm, D_MODEL), row(P_GA // D_MODEL)),
            pl.BlockSpec((tm, D_MODEL), row(P_GB // D_MODEL)),
            pl.BlockSpec((tm, D_MODEL), row(0)),
            pl.BlockSpec((None, 1, D_INNER), par),
            pl.BlockSpec((None, 1, D_MODEL), par),
            pl.BlockSpec((None, D_INNER, D_MODEL), par),
            pl.BlockSpec((None, Q_DIM, D_MODEL), par),
            pl.BlockSpec((None, D_MODEL, D_MODEL), par),
        ],
        out_specs=pl.BlockSpec((tm, D_MODEL), row(0)),
        out_shape=jax.ShapeDtypeStruct((T, D_MODEL), F32),
        compiler_params=pltpu.CompilerParams(
            dimension_semantics=("parallel",),
            vmem_limit_bytes=VMEM_LIMIT_BYTES),
        name="mix",
    )(y2, p2, at2, p2, p2, x2, n_ssd, n_post, w_ssd, w_at, w_out)


FF_CHUNK = D_FF // 2


def _ffn_body(x_ref, npre_ref, npost_ref, win_ref, wout_ref, o_ref):
    x = x_ref[...]
    h = _rms(x, npre_ref[...]).astype(BF16)
    acc = jnp.zeros(x.shape, F32)
    for c0 in range(0, D_FF, FF_CHUNK):
        gate = jnp.dot(h, win_ref[:, c0:c0 + FF_CHUNK], preferred_element_type=F32)
        up = jnp.dot(h, win_ref[:, D_FF + c0:D_FF + c0 + FF_CHUNK], preferred_element_type=F32)
        act = (jax.nn.silu(gate) * up).astype(BF16)
        acc = acc + jnp.dot(act, wout_ref[c0:c0 + FF_CHUNK, :], preferred_element_type=F32)
    o_ref[...] = x + _rms(acc, npost_ref[...])


def _ffn(x2, n_pre, n_post, w_in, w_out, layer, tm):
    T = x2.shape[0]
    par = lambda i: (layer, 0, 0)
    return pl.pallas_call(
        _ffn_body,
        grid=(T // tm,),
        in_specs=[
            pl.BlockSpec((tm, D_MODEL), lambda i: (i, 0)),
            pl.BlockSpec((None, 1, D_MODEL), par),
            pl.BlockSpec((None, 1, D_MODEL), par),
            pl.BlockSpec((None, D_MODEL, 2 * D_FF), par),
            pl.BlockSpec((None, D_FF, D_MODEL), par),
        ],
        out_specs=pl.BlockSpec((tm, D_MODEL), lambda i: (i, 0)),
        out_shape=jax.ShapeDtypeStruct((T, D_MODEL), F32),
        compiler_params=pltpu.CompilerParams(
            dimension_semantics=("parallel",),
            vmem_limit_bytes=VMEM_LIMIT_BYTES),
        name="ffn",
    )(x2, n_pre, n_post, w_in, w_out)


def _tile(total, want):
    t = min(total, want)
    assert total % t == 0, (total, want)
    return t


def kernel(x, pre_mix_norm, w_in, conv_w, conv_b, dt_bias_f, dt_bias_b, A_log_f, A_log_b, D_skip,
           ssd_norm, w_ssd_proj, attn_sink, w_attn_proj, w_out, post_mix_norm, pre_ffn_norm,
           w_ffn_in, w_ffn_out, post_ffn_norm):
    Bsz, L, D = x.shape
    depth = w_in.shape[0]
    assert D == D_MODEL and L % CHUNK == 0 and L >= 3 * WINDOW
    T = Bsz * L

    sizes = dict(z=D_INNER, xs=D_INNER, b=GN, c=GN, dt=2 * N_HEADS, q=Q_DIM, k=KV_DIM, v=KV_DIM,
                 ga=D_MODEL, gb=D_MODEL)
    cols, start = {}, 0
    for name, size in sizes.items():
        cols[name] = w_in[:, :, start:start + size]
        start += size
    assert start == w_in.shape[-1]
    for name in ("k", "v"):
        per_head = cols[name].reshape(depth, D, A_KV, 1, A_DIM)
        cols[name] = jnp.broadcast_to(per_head, (depth, D, A_KV, 2, A_DIM)).reshape(depth, D, 2 * KV_DIM)
    w_main = jnp.concatenate(
        [cols[n] for n in ("z", "xs", "q", "ga", "gb", "b", "c", "k", "v")], axis=-1).astype(BF16)
    assert w_main.shape[-1] == NP
    w_dt = jnp.concatenate([cols["dt"], cols["dt"]], axis=-1).astype(BF16)
    assert w_dt.shape[-1] == DT_PAD
    bias = jnp.concatenate([dt_bias_f, dt_bias_b], axis=-1)
    head_params = jnp.stack(
        [jnp.concatenate([bias, bias], axis=-1),
         jnp.concatenate([jnp.zeros_like(bias), A_log_f, A_log_b], axis=-1)]
        + [jnp.zeros((depth, LANES), F32)] * 6, axis=1)
    dskip = jnp.repeat(D_skip, HEAD_DIM, axis=-1)[:, None, :]
    vec = lambda a: a[:, None, :]
    conv_b3 = vec(conv_b)
    w_ssd_b, w_at_b, w_out_b = (w.astype(BF16) for w in (w_ssd_proj, w_attn_proj, w_out))
    w_fin_b, w_fout_b = w_ffn_in.astype(BF16), w_ffn_out.astype(BF16)

    tm_in = _tile(T, 2048)
    tm_mix = _tile(T, 512)
    tm_ffn = _tile(T, 512)

    x2 = x.reshape(T, D)
    for i in range(depth):
        p2, dt2 = _in_proj(x2, vec(pre_mix_norm), w_main, w_dt, head_params, i, tm_in, 1024)
        p3 = p2.reshape(Bsz, L, NP)
        y3 = _ssd(p3, dt2, conv_w, conv_b3, dskip, i)
        at3 = _attn(p3, attn_sink, i)
        x2 = _mix(y3.reshape(T, D_INNER), p2, at3.reshape(T, Q_DIM), x2, vec(ssd_norm),
                  vec(post_mix_norm), w_ssd_b, w_at_b, w_out_b, i, tm_mix)
        x2 = _ffn(x2, vec(pre_ffn_norm), vec(post_ffn_norm), w_fin_b, w_fout_b, i, tm_ffn)
    return x2.reshape(Bsz, L, D)
```

```python
import functools

import jax
import jax.numpy as jnp
from jax import lax
from jax.experimental import pallas as pl
from jax.experimental.pallas import tpu as pltpu

F32 = jnp.float32
BF16 = jnp.bfloat16
EPS = 1e-6
NEG = -1e30

D_MODEL = 1024
D_INNER = 2048
HEAD_DIM = 64
N_HEADS = 32
N_GROUPS = 4
HEADS_PER_GROUP = N_HEADS // N_GROUPS
D_STATE = 128
GN = N_GROUPS * D_STATE
CONV_DIM = D_INNER + 2 * GN
CONV_W = 5
CHUNK = 128
A_HEADS = 16
A_KV = 4
A_DIM = 64
Q_DIM = A_HEADS * A_DIM
KV_DIM = A_KV * A_DIM
WINDOW = 128
D_FF = 2816

P_Z = 0
P_XS = P_Z + D_INNER
P_Q = P_XS + D_INNER
P_GA = P_Q + Q_DIM
P_GB = P_GA + D_MODEL
P_B = P_GB + D_MODEL
P_C = P_B + GN
P_K = P_C + GN
P_V = P_K + 2 * KV_DIM
NP = P_V + 2 * KV_DIM
DT_PAD = 128

VMEM_LIMIT_BYTES = 56 * 1024 * 1024
LANES = 128


def _rms(x, w):
    ms = jnp.mean(x * x, axis=-1, keepdims=True)
    return x * lax.rsqrt(ms + EPS) * w


def _softplus(x):
    return jnp.maximum(x, 0.0) + jnp.log1p(jnp.exp(-jnp.abs(x)))


def _inproj_body(x_ref, nw_ref, w_ref, wdt_ref, hp_ref, p_ref, dt_ref, h_scr):
    @pl.when(pl.program_id(1) == 0)
    def _():
        h = _rms(x_ref[...], nw_ref[...]).astype(BF16)
        h_scr[...] = h
        raw = jnp.dot(h, wdt_ref[...], preferred_element_type=F32)
        dt = _softplus(raw + hp_ref[0:1, :])
        lane = lax.broadcasted_iota(jnp.int32, (1, DT_PAD), 1)
        dt_ref[...] = (dt * jnp.where(lane < 2 * N_HEADS, 1.0, -jnp.exp(hp_ref[1:2, :]))).T

    p_ref[...] = jnp.dot(h_scr[...], w_ref[...], preferred_element_type=F32).astype(BF16)


def _in_proj(x2, norm_w, w_main, w_dt, head_params, layer, tm, tn):
    T = x2.shape[0]
    return pl.pallas_call(
        _inproj_body,
        grid=(T // tm, NP // tn),
        in_specs=[
            pl.BlockSpec((tm, D_MODEL), lambda i, j: (i, 0)),
            pl.BlockSpec((None, 1, D_MODEL), lambda i, j: (layer, 0, 0)),
            pl.BlockSpec((None, D_MODEL, tn), lambda i, j: (layer, 0, j)),
            pl.BlockSpec((None, D_MODEL, DT_PAD), lambda i, j: (layer, 0, 0)),
            pl.BlockSpec((None, 8, LANES), lambda i, j: (layer, 0, 0)),
        ],
        out_specs=[
            pl.BlockSpec((tm, tn), lambda i, j: (i, j)),
            pl.BlockSpec((DT_PAD, tm), lambda i, j: (0, i)),
        ],
        out_shape=[
            jax.ShapeDtypeStruct((T, NP), BF16),
            jax.ShapeDtypeStruct((DT_PAD, T), F32),
        ],
        scratch_shapes=[pltpu.VMEM((tm, D_MODEL), BF16)],
        compiler_params=pltpu.CompilerParams(
            dimension_semantics=("parallel", "arbitrary"),
            vmem_limit_bytes=VMEM_LIMIT_BYTES),
        name="in_proj",
    )(x2, norm_w, w_main, w_dt, head_params)


HALO = 16
CONV_WIN = CHUNK + 2 * HALO
SIDE_TAPS = tuple(k for k in range(CONV_W) if k != CONV_W // 2)


def _shift_matrix():
    row = lax.broadcasted_iota(jnp.int32, (len(SIDE_TAPS) * CHUNK, CONV_WIN), 0)
    col = lax.broadcasted_iota(jnp.int32, (len(SIDE_TAPS) * CHUNK, CONV_WIN), 1)
    blk = row // CHUNK
    tap = jnp.where(blk < CONV_W // 2, blk, blk + 1)
    return (col == (row - blk * CHUNK) + HALO - CONV_W // 2 + tap).astype(F32).astype(BF16)


def _conv_silu_chunk(streams, shift_ref, c, nc):
    r0 = pl.multiple_of(c * CHUNK, CHUNK)
    prev0 = pl.multiple_of(jnp.maximum(r0 - HALO, 0), HALO)
    next0 = pl.multiple_of(jnp.minimum(r0 + CHUNK, (nc - 1) * CHUNK + CHUNK - HALO), HALO)

    def rows(start, n):
        return jnp.concatenate([src_ref[0, pl.ds(start, n), :] for src_ref, _, _, _ in streams], axis=1)

    prev, main, nxt = rows(prev0, HALO), rows(r0, CHUNK), rows(next0, HALO)
    prev = jnp.where(c > 0, prev, jnp.zeros_like(prev))
    nxt = jnp.where(c < nc - 1, nxt, jnp.zeros_like(nxt))
    win = jnp.concatenate([prev, main, nxt], axis=0)
    shifted = jnp.dot(shift_ref[...], win, preferred_element_type=F32)
    col = 0
    for _, w_ref, b_ref, dst_ref in streams:
        width = dst_ref.shape[-1]
        acc = b_ref[...] + main[:, col:col + width].astype(F32) * w_ref[CONV_W // 2:CONV_W // 2 + 1, :]
        for i, k in enumerate(SIDE_TAPS):
            acc = acc + shifted[i * CHUNK:(i + 1) * CHUNK, col:col + width] * w_ref[k:k + 1, :]
        dst_ref[pl.ds(r0, CHUNK), :] = (acc * jax.nn.sigmoid(acc)).astype(BF16)
        col += width


def _lane_cumsum(x, reverse):
    n = x.shape[1]
    lane = lax.broadcasted_iota(jnp.int32, x.shape, 1) & (LANES - 1)
    k = 1
    while k < LANES:
        if reverse:
            sh = pltpu.roll(x, n - k, axis=1)
            x = x + jnp.where(lane < LANES - k, sh, 0.0)
        else:
            sh = pltpu.roll(x, k, axis=1)
            x = x + jnp.where(lane >= k, sh, 0.0)
        k *= 2
    return x


ROW_CUM, ROW_SRC, ROW_W, ROW_CD, ROWS_PER_DIR = 0, 8, 16, 24, 32


def _scan_tables(dta_ref, pre_s):
    log2e = 1.4426950408889634
    g8 = pl.program_id(1) * HEADS_PER_GROUP
    for d in range(2):
        dt = dta_ref[pl.ds(pl.multiple_of(d * N_HEADS + g8, 8), 8), :]
        a = dta_ref[pl.ds(pl.multiple_of((2 + d) * N_HEADS + g8, 8), 8), :]
        inc = _lane_cumsum(a, False)
        rev = _lane_cumsum(a, True)
        cum, other = (inc, rev) if d == 0 else (rev, inc)
        base = d * ROWS_PER_DIR
        pre_s[base + ROW_CUM:base + ROW_CUM + 8, :] = cum * log2e
        pre_s[base + ROW_SRC:base + ROW_SRC + 8, :] = (cum - jnp.log(dt)) * log2e
        pre_s[base + ROW_W:base + ROW_W + 8, :] = dt * jnp.exp(other - a)
        pre_s[base + ROW_CD:base + ROW_CD + 8, :] = jnp.exp(inc + rev - a)


def _stage_operands(c, reverse, slot, refs):
    (pre_s, xs_s, b_s, c_s, h_s, lhs_s, btw_s) = refs
    r0 = pl.multiple_of(c * CHUNK, CHUNK)
    base = ROWS_PER_DIR if reverse else 0
    cum8 = pre_s[base + ROW_CUM:base + ROW_CUM + 8, pl.ds(r0, CHUNK)]
    src8 = pre_s[base + ROW_SRC:base + ROW_SRC + 8, pl.ds(r0, CHUNK)]
    w8 = pre_s[base + ROW_W:base + ROW_W + 8, pl.ds(r0, CHUNK)]
    bc = b_s[pl.ds(r0, CHUNK), :]
    cc = c_s[pl.ds(r0, CHUNK), :]
    cb = lax.dot_general(cc, bc, (((1,), (1,)), ((), ())), preferred_element_type=F32).astype(BF16)
    bT = bc.astype(F32).T.astype(BF16)
    li = lax.broadcasted_iota(jnp.int32, (CHUNK, CHUNK), 0)
    si = lax.broadcasted_iota(jnp.int32, (CHUNK, CHUNK), 1)
    mask = (si >= li) if reverse else (si <= li)
    for j in range(HEADS_PER_GROUP):
        acol = jnp.broadcast_to(cum8[j:j + 1, :], (CHUNK, CHUNK)).T
        dec = jnp.exp2(jnp.where(mask, acol - src8[j:j + 1, :], NEG))
        lhs_s[slot, j, :, 0:CHUNK] = cb * dec.astype(BF16)
        lhs_s[slot, j, :, CHUNK:CHUNK + D_STATE] = cc * jnp.exp2(acol).astype(BF16)
        btw_s[slot, j] = bT * w8[j:j + 1, :].astype(BF16)


def _apply_chunk(c, reverse, slot, refs):
    (pre_s, xs_s, b_s, c_s, h_s, lhs_s, btw_s) = refs
    r0 = pl.multiple_of(c * CHUNK, CHUNK)
    base = ROWS_PER_DIR if reverse else 0
    cd8 = pre_s[base + ROW_CD:base + ROW_CD + 8, pl.ds(r0, CHUNK)]
    xs = xs_s[pl.ds(r0, CHUNK), :]
    low = lax.broadcasted_iota(jnp.int32, (CHUNK, LANES), 1) < HEAD_DIM
    y_parts = []
    for p in range(HEADS_PER_GROUP // 2):
        xp = xs[:, p * LANES:(p + 1) * LANES]
        hp_state = h_s[:, p * LANES:(p + 1) * LANES]
        rhs = jnp.concatenate([xp, hp_state.astype(BF16)], axis=0)
        res = [jnp.dot(lhs_s[slot, 2 * p + e], rhs, preferred_element_type=F32) for e in range(2)]
        st = [jnp.dot(btw_s[slot, 2 * p + e], xp, preferred_element_type=F32) for e in range(2)]
        y_parts.append(jnp.where(low, res[0], res[1]))
        cd_pair = jnp.where(low[0:1, :], cd8[2 * p:2 * p + 1, :], cd8[2 * p + 1:2 * p + 2, :])
        h_s[:, p * LANES:(p + 1) * LANES] = hp_state * cd_pair + jnp.where(low, st[0], st[1])
    return jnp.concatenate(y_parts, axis=1)


def _ssd_body(xs_ref, b_ref, c_ref, dta_ref, cwx_ref, cwb_ref, cwc_ref, cbx_ref, cbb_ref, cbc_ref,
              dsk_ref, y_ref,
              xs_s, b_s, c_s, shift_s, pre_s, yacc_s, h_s, lhs_s, btw_s, *, nc):
    _scan_tables(dta_ref, pre_s)
    shift_s[...] = _shift_matrix()

    def conv_step(c, carry):
        _conv_silu_chunk([(xs_ref, cwx_ref, cbx_ref, xs_s)], shift_s, c, nc)
        _conv_silu_chunk([(b_ref, cwb_ref, cbb_ref, b_s), (c_ref, cwc_ref, cbc_ref, c_s)], shift_s, c, nc)
        return carry

    lax.fori_loop(0, nc, conv_step, 0, unroll=4)

    refs = (pre_s, xs_s, b_s, c_s, h_s, lhs_s, btw_s)

    def scan(reverse, emit):
        chunk = (lambda k: nc - 1 - k) if reverse else (lambda k: k)
        h_s[...] = jnp.zeros_like(h_s)
        _stage_operands(chunk(0), reverse, 0, refs)

        def step(t, carry):
            k = 2 * t
            _stage_operands(chunk(k + 1), reverse, 1, refs)
            emit(chunk(k), _apply_chunk(chunk(k), reverse, 0, refs))
            _stage_operands(chunk(jnp.minimum(k + 2, nc - 1)), reverse, 0, refs)
            emit(chunk(k + 1), _apply_chunk(chunk(k + 1), reverse, 1, refs))
            return carry

        lax.fori_loop(0, nc // 2, step, 0)

    def emit_fwd(c, y):
        r0 = pl.multiple_of(c * CHUNK, CHUNK)
        yacc_s[pl.ds(r0, CHUNK), :] = y + dsk_ref[...] * xs_s[pl.ds(r0, CHUNK), :].astype(F32)

    def emit_bwd(c, y):
        r0 = pl.multiple_of(c * CHUNK, CHUNK)
        y_ref[0, pl.ds(r0, CHUNK), :] = (yacc_s[pl.ds(r0, CHUNK), :] + y).astype(BF16)

    scan(False, emit_fwd)
    scan(True, emit_bwd)


def _ssd(p3, dta, conv_w, conv_b, dskip, layer):
    Bsz, L, _ = p3.shape
    nc = L // CHUNK
    gw = HEADS_PER_GROUP * HEAD_DIM
    xs_blk, b_blk, c_blk = P_XS // gw, P_B // D_STATE, P_C // D_STATE
    cw_b0, cw_c0 = D_INNER // D_STATE, (D_INNER + GN) // D_STATE
    return pl.pallas_call(
        functools.partial(_ssd_body, nc=nc),
        grid=(Bsz, N_GROUPS),
        in_specs=[
            pl.BlockSpec((1, L, gw), lambda b, g: (b, 0, xs_blk + g)),
            pl.BlockSpec((1, L, D_STATE), lambda b, g: (b, 0, b_blk + g)),
            pl.BlockSpec((1, L, D_STATE), lambda b, g: (b, 0, c_blk + g)),
            pl.BlockSpec((DT_PAD, L), lambda b, g: (0, b)),
            pl.BlockSpec((None, CONV_W, gw), lambda b, g: (layer, 0, g)),
            pl.BlockSpec((None, CONV_W, D_STATE), lambda b, g: (layer, 0, cw_b0 + g)),
            pl.BlockSpec((None, CONV_W, D_STATE), lambda b, g: (layer, 0, cw_c0 + g)),
            pl.BlockSpec((None, 1, gw), lambda b, g: (layer, 0, g)),
            pl.BlockSpec((None, 1, D_STATE), lambda b, g: (layer, 0, cw_b0 + g)),
            pl.BlockSpec((None, 1, D_STATE), lambda b, g: (layer, 0, cw_c0 + g)),
            pl.BlockSpec((None, 1, gw), lambda b, g: (layer, 0, g)),
        ],
        out_specs=pl.BlockSpec((1, L, gw), lambda b, g: (b, 0, g)),
        out_shape=jax.ShapeDtypeStruct((Bsz, L, D_INNER), BF16),
        scratch_shapes=[
            pltpu.VMEM((L, gw), BF16),
            pltpu.VMEM((L, D_STATE), BF16),
            pltpu.VMEM((L, D_STATE), BF16),
            pltpu.VMEM((len(SIDE_TAPS) * CHUNK, CONV_WIN), BF16),
            pltpu.VMEM((2 * ROWS_PER_DIR, L), F32),
            pltpu.VMEM((L, gw), F32),
            pltpu.VMEM((D_STATE, gw), F32),
            pltpu.VMEM((2, HEADS_PER_GROUP, CHUNK, CHUNK + D_STATE), BF16),
            pltpu.VMEM((2, HEADS_PER_GROUP, D_STATE, CHUNK), BF16),
        ],
        compiler_params=pltpu.CompilerParams(
            dimension_semantics=("parallel", "parallel"),
            vmem_limit_bytes=VMEM_LIMIT_BYTES),
        name="ssd",
    )(p3, p3, p3, dta, conv_w, conv_w, conv_w, conv_b, conv_b, conv_b, dskip)


KV_ROWS = 3 * WINDOW
KV_DUP = 2 * KV_DIM
assert A_DIM ** -0.5 == 0.125


def _attn_body(sink_ref, q_ref, k_ref, v_ref, o_ref, bias_s, s_s, p_s, vt_s, *, layer, nb):
    i = pl.program_id(1)
    start = pl.multiple_of(jnp.clip(i - 1, 0, nb - 3) * WINDOW, WINDOW)
    heads_per_kv = A_HEADS // A_KV
    grp_lanes = heads_per_kv * WINDOW

    @pl.when((i <= 1) | (i == nb - 1))
    def _():
        off = i * WINDOW - start
        key = lax.broadcasted_iota(jnp.int32, (KV_ROWS, WINDOW), 0)
        qry = lax.broadcasted_iota(jnp.int32, (KV_ROWS, WINDOW), 1)
        dist = jnp.abs(off + qry - key)
        valid = dist <= WINDOW
        distf = dist.astype(F32)
        for h in range(A_HEADS):
            slope = 2.0 ** (-8.0 * (h + 1) / A_HEADS)
            bias_s[h // heads_per_kv, :, (h % heads_per_kv) * WINDOW:(h % heads_per_kv + 1) * WINDOW] = (
                jnp.where(valid, -slope * distf, NEG))

    low_q = lax.broadcasted_iota(jnp.int32, (WINDOW, LANES), 1) < A_DIM
    col_head = lax.broadcasted_iota(jnp.int32, (1, grp_lanes), 1) // WINDOW
    top = lax.broadcasted_iota(jnp.int32, (LANES, KV_ROWS), 0) < A_DIM
    trans_b = (((1,), (1,)), ((), ()))

    for g in range(A_KV):
        parts = []
        for pr in range(2 * g, 2 * g + 2):
            qp = q_ref[0, :, pr * LANES:(pr + 1) * LANES] * 0.125
            zero = jnp.zeros_like(qp)
            parts += [jnp.where(low_q, qp, zero), jnp.where(low_q, zero, qp)]
        kd = k_ref[0, pl.ds(start, KV_ROWS), g * LANES:(g + 1) * LANES]
        s_s[g] = lax.dot_general(kd, jnp.concatenate(parts, axis=0), trans_b, preferred_element_type=F32)
        vt = v_ref[0, pl.ds(start, KV_ROWS), g * LANES:(g + 1) * LANES].astype(F32).T
        vt_s[g] = jnp.where(top, vt, 1.0).astype(BF16)

    maxima, sinks = [], []
    for g in range(A_KV):
        sink = jnp.zeros((1, grp_lanes), F32)
        for e in range(heads_per_kv):
            sink = jnp.where(col_head == e, sink_ref[layer, g * heads_per_kv + e], sink)
        s = s_s[g] + bias_s[g]
        m = jnp.maximum(jnp.max(s, axis=0, keepdims=True), sink)
        p_s[g] = jnp.exp(s - m).astype(BF16)
        maxima.append(m)
        sinks.append(sink)

    for g in range(A_KV):
        res = jnp.dot(vt_s[g], p_s[g], preferred_element_type=F32)
        o_t = res[0:A_DIM] / (res[A_DIM:2 * A_DIM] + jnp.exp(sinks[g] - maxima[g]))
        o_grp = jnp.concatenate([o_t[:, e * WINDOW:(e + 1) * WINDOW] for e in range(heads_per_kv)], axis=0)
        o_ref[0, :, g * 2 * LANES:(g + 1) * 2 * LANES] = o_grp.T.astype(BF16)


def _attn(p3, sink, layer):
    Bsz, L, _ = p3.shape
    nb = L // WINDOW
    return pl.pallas_call(
        functools.partial(_attn_body, layer=layer, nb=nb),
        grid=(Bsz, nb),
        in_specs=[
            pl.BlockSpec(memory_space=pltpu.SMEM),
            pl.BlockSpec((1, WINDOW, Q_DIM), lambda b, i: (b, i, P_Q // Q_DIM)),
            pl.BlockSpec((1, L, KV_DUP), lambda b, i: (b, 0, P_K // KV_DUP)),
            pl.BlockSpec((1, L, KV_DUP), lambda b, i: (b, 0, P_V // KV_DUP)),
        ],
        out_specs=pl.BlockSpec((1, WINDOW, Q_DIM), lambda b, i: (b, i, 0)),
        out_shape=jax.ShapeDtypeStruct((Bsz, L, Q_DIM), BF16),
        scratch_shapes=[
            pltpu.VMEM((A_KV, KV_ROWS, 4 * WINDOW), F32),
            pltpu.VMEM((A_KV, KV_ROWS, 4 * WINDOW), F32),
            pltpu.VMEM((A_KV, KV_ROWS, 4 * WINDOW), BF16),
            pltpu.VMEM((A_KV, LANES, KV_ROWS), BF16),
        ],
        compiler_params=pltpu.CompilerParams(
            dimension_semantics=("parallel", "arbitrary"),
            vmem_limit_bytes=VMEM_LIMIT_BYTES),
        name="attn",
    )(sink, p3, p3, p3)


def _mix_body(y_ref, z_ref, at_ref, ga_ref, gb_ref, x_ref, nssd_ref, npost_ref,
              wssd_ref, wat_ref, wout_ref, o_ref):
    u = y_ref[...].astype(F32) * jax.nn.silu(z_ref[...].astype(F32))
    un = _rms(u, nssd_ref[...]).astype(BF16)
    y_ssd = jnp.dot(un, wssd_ref[...], preferred_element_type=F32)
    y_at = jnp.dot(at_ref[...], wat_ref[...], preferred_element_type=F32)
    mix_in = (jax.nn.sigmoid(ga_ref[...].astype(F32)) * y_ssd
              + jax.nn.sigmoid(gb_ref[...].astype(F32)) * y_at).astype(BF16)
    mix = jnp.dot(mix_in, wout_ref[...], preferred_element_type=F32)
    o_ref[...] = x_ref[...] + _rms(mix, npost_ref[...])


def _mix(y2, p2, at2, x2, n_ssd, n_post, w_ssd, w_at, w_out, layer, tm):
    T = x2.shape[0]
    row = lambda blk: (lambda i: (i, blk))
    par = lambda i: (layer, 0, 0)
    return pl.pallas_call(
        _mix_body,
        grid=(T // tm,),
        in_specs=[
            pl.BlockSpec((tm, D_INNER), row(0)),
            pl.BlockSpec((tm, D_INNER), row(P_Z // D_INNER)),
            pl.BlockSpec((tm, Q_DIM), row(0)),
            pl.BlockSpec((tm, D_MODEL), row(P_GA // D_MODEL)),
            pl.BlockSpec((tm, D_MODEL), row(P_GB // D_MODEL)),
            pl.BlockSpec((tm, D_MODEL), row(0)),
            pl.BlockSpec((None, 1, D_INNER), par),
            pl.BlockSpec((None, 1, D_MODEL), par),
            pl.BlockSpec((None, D_INNER, D_MODEL), par),
            pl.BlockSpec((None, Q_DIM, D_MODEL), par),
            pl.BlockSpec((None, D_MODEL, D_MODEL), par),
        ],
        out_specs=pl.BlockSpec((tm, D_MODEL), row(0)),
        out_shape=jax.ShapeDtypeStruct((T, D_MODEL), F32),
        compiler_params=pltpu.CompilerParams(
            dimension_semantics=("parallel",),
            vmem_limit_bytes=VMEM_LIMIT_BYTES),
        name="mix",
    )(y2, p2, at2, p2, p2, x2, n_ssd, n_post, w_ssd, w_at, w_out)


FF_CHUNK = D_FF // 2


def _ffn_body(x_ref, npre_ref, npost_ref, win_ref, wout_ref, o_ref):
    x = x_ref[...]
    h = _rms(x, npre_ref[...]).astype(BF16)
    acc = jnp.zeros(x.shape, F32)
    for c0 in range(0, D_FF, FF_CHUNK):
        gate = jnp.dot(h, win_ref[:, c0:c0 + FF_CHUNK], preferred_element_type=F32)
        up = jnp.dot(h, win_ref[:, D_FF + c0:D_FF + c0 + FF_CHUNK], preferred_element_type=F32)
        act = (jax.nn.silu(gate) * up).astype(BF16)
        acc = acc + jnp.dot(act, wout_ref[c0:c0 + FF_CHUNK, :], preferred_element_type=F32)
    o_ref[...] = x + _rms(acc, npost_ref[...])


def _ffn(x2, n_pre, n_post, w_in, w_out, layer, tm):
    T = x2.shape[0]
    par = lambda i: (layer, 0, 0)
    return pl.pallas_call(
        _ffn_body,
        grid=(T // tm,),
        in_specs=[
            pl.BlockSpec((tm, D_MODEL), lambda i: (i, 0)),
            pl.BlockSpec((None, 1, D_MODEL), par),
            pl.BlockSpec((None, 1, D_MODEL), par),
            pl.BlockSpec((None, D_MODEL, 2 * D_FF), par),
            pl.BlockSpec((None, D_FF, D_MODEL), par),
        ],
        out_specs=pl.BlockSpec((tm, D_MODEL), lambda i: (i, 0)),
        out_shape=jax.ShapeDtypeStruct((T, D_MODEL), F32),
        compiler_params=pltpu.CompilerParams(
            dimension_semantics=("parallel",),
            vmem_limit_bytes=VMEM_LIMIT_BYTES),
        name="ffn",
    )(x2, n_pre, n_post, w_in, w_out)


def _tile(total, want):
    t = min(total, want)
    assert total % t == 0, (total, want)
    return t


def kernel(x, pre_mix_norm, w_in, conv_w, conv_b, dt_bias_f, dt_bias_b, A_log_f, A_log_b, D_skip,
           ssd_norm, w_ssd_proj, attn_sink, w_attn_proj, w_out, post_mix_norm, pre_ffn_norm,
           w_ffn_in, w_ffn_out, post_ffn_norm):
    Bsz, L, D = x.shape
    depth = w_in.shape[0]
    assert D == D_MODEL and L % CHUNK == 0 and L >= 3 * WINDOW
    T = Bsz * L

    sizes = dict(z=D_INNER, xs=D_INNER, b=GN, c=GN, dt=2 * N_HEADS, q=Q_DIM, k=KV_DIM, v=KV_DIM,
                 ga=D_MODEL, gb=D_MODEL)
    cols, start = {}, 0
    for name, size in sizes.items():
        cols[name] = w_in[:, :, start:start + size]
        start += size
    assert start == w_in.shape[-1]
    for name in ("k", "v"):
        per_head = cols[name].reshape(depth, D, A_KV, 1, A_DIM)
        cols[name] = jnp.broadcast_to(per_head, (depth, D, A_KV, 2, A_DIM)).reshape(depth, D, 2 * KV_DIM)
    w_main = jnp.concatenate(
        [cols[n] for n in ("z", "xs", "q", "ga", "gb", "b", "c", "k", "v")], axis=-1).astype(BF16)
    assert w_main.shape[-1] == NP
    w_dt = jnp.concatenate([cols["dt"], cols["dt"]], axis=-1).astype(BF16)
    assert w_dt.shape[-1] == DT_PAD
    bias = jnp.concatenate([dt_bias_f, dt_bias_b], axis=-1)
    head_params = jnp.stack(
        [jnp.concatenate([bias, bias], axis=-1),
         jnp.concatenate([jnp.zeros_like(bias), A_log_f, A_log_b], axis=-1)]
        + [jnp.zeros((depth, LANES), F32)] * 6, axis=1)
    dskip = jnp.repeat(D_skip, HEAD_DIM, axis=-1)[:, None, :]
    vec = lambda a: a[:, None, :]
    conv_b3 = vec(conv_b)
    w_ssd_b, w_at_b, w_out_b = (w.astype(BF16) for w in (w_ssd_proj, w_attn_proj, w_out))
    w_fin_b, w_fout_b = w_ffn_in.astype(BF16), w_ffn_out.astype(BF16)

    tm_in = _tile(T, 2048)
    tm_mix = _tile(T, 512)
    tm_ffn = _tile(T, 512)

    x2 = x.reshape(T, D)
    for i in range(depth):
        p2, dt2 = _in_proj(x2, vec(pre_mix_norm), w_main, w_dt, head_params, i, tm_in, 1024)
        p3 = p2.reshape(Bsz, L, NP)
        y3 = _ssd(p3, dt2, conv_w, conv_b3, dskip, i)
        at3 = _attn(p3, attn_sink, i)
        x2 = _mix(y3.reshape(T, D_INNER), p2, at3.reshape(T, Q_DIM), x2, vec(ssd_norm),
                  vec(post_mix_norm), w_ssd_b, w_at_b, w_out_b, i, tm_mix)
        x2 = _ffn(x2, vec(pre_ffn_norm), vec(post_ffn_norm), w_fin_b, w_fout_b, i, tm_ffn)
    return x2.reshape(Bsz, L, D)
```

```python
import functools

import jax
import jax.numpy as jnp
from jax import lax
from jax.experimental import pallas as pl
from jax.experimental.pallas import tpu as pltpu

F32 = jnp.float32
BF16 = jnp.bfloat16
EPS = 1e-6
NEG = -1e30

D_MODEL = 1024
D_INNER = 2048
HEAD_DIM = 64
N_HEADS = 32
N_GROUPS = 4
HEADS_PER_GROUP = N_HEADS // N_GROUPS
D_STATE = 128
GN = N_GROUPS * D_STATE
CONV_DIM = D_INNER + 2 * GN
CONV_W = 5
CHUNK = 128
A_HEADS = 16
A_KV = 4
A_DIM = 64
Q_DIM = A_HEADS * A_DIM
KV_DIM = A_KV * A_DIM
WINDOW = 128
D_FF = 2816

P_Z = 0
P_XS = P_Z + D_INNER
P_Q = P_XS + D_INNER
P_GA = P_Q + Q_DIM
P_GB = P_GA + D_MODEL
P_B = P_GB + D_MODEL
P_C = P_B + GN
P_K = P_C + GN
P_V = P_K + 2 * KV_DIM
NP = P_V + 2 * KV_DIM
DT_PAD = 128

VMEM_LIMIT_BYTES = 56 * 1024 * 1024
LANES = 128


def _rms(x, w):
    ms = jnp.mean(x * x, axis=-1, keepdims=True)
    return x * lax.rsqrt(ms + EPS) * w


def _softplus(x):
    return jnp.maximum(x, 0.0) + jnp.log1p(jnp.exp(-jnp.abs(x)))


def _inproj_body(x_ref, nw_ref, w_ref, wdt_ref, hp_ref, p_ref, dt_ref, h_scr):
    @pl.when(pl.program_id(1) == 0)
    def _():
        h = _rms(x_ref[...], nw_ref[...]).astype(BF16)
        h_scr[...] = h
        raw = jnp.dot(h, wdt_ref[...], preferred_element_type=F32)
        dt = _softplus(raw + hp_ref[0:1, :])
        lane = lax.broadcasted_iota(jnp.int32, (1, DT_PAD), 1)
        dt_ref[...] = (dt * jnp.where(lane < 2 * N_HEADS, 1.0, -jnp.exp(hp_ref[1:2, :]))).T

    p_ref[...] = jnp.dot(h_scr[...], w_ref[...], preferred_element_type=F32).astype(BF16)


def _in_proj(x2, norm_w, w_main, w_dt, head_params, layer, tm, tn):
    T = x2.shape[0]
    return pl.pallas_call(
        _inproj_body,
        grid=(T // tm, NP // tn),
        in_specs=[
            pl.BlockSpec((tm, D_MODEL), lambda i, j: (i, 0)),
            pl.BlockSpec((None, 1, D_MODEL), lambda i, j: (layer, 0, 0)),
            pl.BlockSpec((None, D_MODEL, tn), lambda i, j: (layer, 0, j)),
            pl.BlockSpec((None, D_MODEL, DT_PAD), lambda i, j: (layer, 0, 0)),
            pl.BlockSpec((None, 8, LANES), lambda i, j: (layer, 0, 0)),
        ],
        out_specs=[
            pl.BlockSpec((tm, tn), lambda i, j: (i, j)),
            pl.BlockSpec((DT_PAD, tm), lambda i, j: (0, i)),
        ],
        out_shape=[
            jax.ShapeDtypeStruct((T, NP), BF16),
            jax.ShapeDtypeStruct((DT_PAD, T), F32),
        ],
        scratch_shapes=[pltpu.VMEM((tm, D_MODEL), BF16)],
        compiler_params=pltpu.CompilerParams(
            dimension_semantics=("parallel", "arbitrary"),
            vmem_limit_bytes=VMEM_LIMIT_BYTES),
        name="in_proj",
    )(x2, norm_w, w_main, w_dt, head_params)


HALO = 16
CONV_WIN = CHUNK + 2 * HALO
SIDE_TAPS = tuple(k for k in range(CONV_W) if k != CONV_W // 2)


def _shift_matrix():
    row = lax.broadcasted_iota(jnp.int32, (len(SIDE_TAPS) * CHUNK, CONV_WIN), 0)
    col = lax.broadcasted_iota(jnp.int32, (len(SIDE_TAPS) * CHUNK, CONV_WIN), 1)
    blk = row // CHUNK
    tap = jnp.where(blk < CONV_W // 2, blk, blk + 1)
    return (col == (row - blk * CHUNK) + HALO - CONV_W // 2 + tap).astype(F32).astype(BF16)


def _conv_silu_chunk(streams, shift_ref, c, nc):
    r0 = pl.multiple_of(c * CHUNK, CHUNK)
    prev0 = pl.multiple_of(jnp.maximum(r0 - HALO, 0), HALO)
    next0 = pl.multiple_of(jnp.minimum(r0 + CHUNK, (nc - 1) * CHUNK + CHUNK - HALO), HALO)

    def rows(start, n):
        return jnp.concatenate([src_ref[0, pl.ds(start, n), :] for src_ref, _, _, _ in streams], axis=1)

    prev, main, nxt = rows(prev0, HALO), rows(r0, CHUNK), rows(next0, HALO)
    prev = jnp.where(c > 0, prev, jnp.zeros_like(prev))
    nxt = jnp.where(c < nc - 1, nxt, jnp.zeros_like(nxt))
    win = jnp.concatenate([prev, main, nxt], axis=0)
    shifted = jnp.dot(shift_ref[...], win, preferred_element_type=F32)
    col = 0
    for _, w_ref, b_ref, dst_ref in streams:
        width = dst_ref.shape[-1]
        acc = b_ref[...] + main[:, col:col + width].astype(F32) * w_ref[CONV_W // 2:CONV_W // 2 + 1, :]
        for i, k in enumerate(SIDE_TAPS):
            acc = acc + shifted[i * CHUNK:(i + 1) * CHUNK, col:col + width] * w_ref[k:k + 1, :]
        dst_ref[pl.ds(r0, CHUNK), :] = (acc * jax.nn.sigmoid(acc)).astype(BF16)
        col += width


def _lane_cumsum(x, reverse):
    n = x.shape[1]
    lane = lax.broadcasted_iota(jnp.int32, x.shape, 1) & (LANES - 1)
    k = 1
    while k < LANES:
        if reverse:
            sh = pltpu.roll(x, n - k, axis=1)
            x = x + jnp.where(lane < LANES - k, sh, 0.0)
        else:
            sh = pltpu.roll(x, k, axis=1)
            x = x + jnp.where(lane >= k, sh, 0.0)
        k *= 2
    return x


ROW_CUM, ROW_SRC, ROW_W, ROW_CD, ROWS_PER_DIR = 0, 8, 16, 24, 32


def _scan_tables(dta_ref, pre_s):
    log2e = 1.4426950408889634
    g8 = pl.program_id(1) * HEADS_PER_GROUP
    for d in range(2):
        dt = dta_ref[pl.ds(pl.multiple_of(d * N_HEADS + g8, 8), 8), :]
        a = dta_ref[pl.ds(pl.multiple_of((2 + d) * N_HEADS + g8, 8), 8), :]
        inc = _lane_cumsum(a, False)
        rev = _lane_cumsum(a, True)
        cum, other = (inc, rev) if d == 0 else (rev, inc)
        base = d * ROWS_PER_DIR
        pre_s[base + ROW_CUM:base + ROW_CUM + 8, :] = cum * log2e
        pre_s[base + ROW_SRC:base + ROW_SRC + 8, :] = (cum - jnp.log(dt)) * log2e
        pre_s[base + ROW_W:base + ROW_W + 8, :] = dt * jnp.exp(other - a)
        pre_s[base + ROW_CD:base + ROW_CD + 8, :] = jnp.exp(inc + rev - a)


def _stage_operands(c, reverse, slot, refs):
    (pre_s, xs_s, b_s, c_s, h_s, lhs_s, btw_s) = refs
    r0 = pl.multiple_of(c * CHUNK, CHUNK)
    base = ROWS_PER_DIR if reverse else 0
    cum8 = pre_s[base + ROW_CUM:base + ROW_CUM + 8, pl.ds(r0, CHUNK)]
    src8 = pre_s[base + ROW_SRC:base + ROW_SRC + 8, pl.ds(r0, CHUNK)]
    w8 = pre_s[base + ROW_W:base + ROW_W + 8, pl.ds(r0, CHUNK)]
    bc = b_s[pl.ds(r0, CHUNK), :]
    cc = c_s[pl.ds(r0, CHUNK), :]
    cb = lax.dot_general(cc, bc, (((1,), (1,)), ((), ())), preferred_element_type=F32).astype(BF16)
    bT = bc.astype(F32).T.astype(BF16)
    li = lax.broadcasted_iota(jnp.int32, (CHUNK, CHUNK), 0)
    si = lax.broadcasted_iota(jnp.int32, (CHUNK, CHUNK), 1)
    mask = (si >= li) if reverse else (si <= li)
    for j in range(HEADS_PER_GROUP):
        acol = jnp.broadcast_to(cum8[j:j + 1, :], (CHUNK, CHUNK)).T
        dec = jnp.exp2(jnp.where(mask, acol - src8[j:j + 1, :], NEG))
        lhs_s[slot, j, :, 0:CHUNK] = cb * dec.astype(BF16)
        lhs_s[slot, j, :, CHUNK:CHUNK + D_STATE] = cc * jnp.exp2(acol).astype(BF16)
        btw_s[slot, j] = bT * w8[j:j + 1, :].astype(BF16)


def _apply_chunk(c, reverse, slot, refs):
    (pre_s, xs_s, b_s, c_s, h_s, lhs_s, btw_s) = refs
    r0 = pl.multiple_of(c * CHUNK, CHUNK)
    base = ROWS_PER_DIR if reverse else 0
    cd8 = pre_s[base + ROW_CD:base + ROW_CD + 8, pl.ds(r0, CHUNK)]
    xs = xs_s[pl.ds(r0, CHUNK), :]
    low = lax.broadcasted_iota(jnp.int32, (CHUNK, LANES), 1) < HEAD_DIM
    y_parts = []
    for p in range(HEADS_PER_GROUP // 2):
        xp = xs[:, p * LANES:(p + 1) * LANES]
        hp_state = h_s[:, p * LANES:(p + 1) * LANES]
        rhs = jnp.concatenate([xp, hp_state.astype(BF16)], axis=0)
        res = [jnp.dot(lhs_s[slot, 2 * p + e], rhs, preferred_element_type=F32) for e in range(2)]
        st = [jnp.dot(btw_s[slot, 2 * p + e], xp, preferred_element_type=F32) for e in range(2)]
        y_parts.append(jnp.where(low, res[0], res[1]))
        cd_pair = jnp.where(low[0:1, :], cd8[2 * p:2 * p + 1, :], cd8[2 * p + 1:2 * p + 2, :])
        h_s[:, p * LANES:(p + 1) * LANES] = hp_state * cd_pair + jnp.where(low, st[0], st[1])
    return jnp.concatenate(y_parts, axis=1)


def _ssd_body(xs_ref, b_ref, c_ref, dta_ref, cwx_ref, cwb_ref, cwc_ref, cbx_ref, cbb_ref, cbc_ref,
              dsk_ref, y_ref,
              xs_s, b_s, c_s, shift_s, pre_s, yacc_s, h_s, lhs_s, btw_s, *, nc):
    _scan_tables(dta_ref, pre_s)
    shift_s[...] = _shift_matrix()

    def conv_step(c, carry):
        _conv_silu_chunk([(xs_ref, cwx_ref, cbx_ref, xs_s)], shift_s, c, nc)
        _conv_silu_chunk([(b_ref, cwb_ref, cbb_ref, b_s), (c_ref, cwc_ref, cbc_ref, c_s)], shift_s, c, nc)
        return carry

    lax.fori_loop(0, nc, conv_step, 0, unroll=4)

    refs = (pre_s, xs_s, b_s, c_s, h_s, lhs_s, btw_s)

    def scan(reverse, emit):
        chunk = (lambda k: nc - 1 - k) if reverse else (lambda k: k)
        h_s[...] = jnp.zeros_like(h_s)
        _stage_operands(chunk(0), reverse, 0, refs)

        def step(t, carry):
            k = 2 * t
            _stage_operands(chunk(k + 1), reverse, 1, refs)
            emit(chunk(k), _apply_chunk(chunk(k), reverse, 0, refs))
            _stage_operands(chunk(jnp.minimum(k + 2, nc - 1)), reverse, 0, refs)
            emit(chunk(k + 1), _apply_chunk(chunk(k + 1), reverse, 1, refs))
            return carry

        lax.fori_loop(0, nc // 2, step, 0, unroll=min(4, nc // 2))

    def emit_fwd(c, y):
        r0 = pl.multiple_of(c * CHUNK, CHUNK)
        yacc_s[pl.ds(r0, CHUNK), :] = y + dsk_ref[...] * xs_s[pl.ds(r0, CHUNK), :].astype(F32)

    def emit_bwd(c, y):
        r0 = pl.multiple_of(c * CHUNK, CHUNK)
        y_ref[0, pl.ds(r0, CHUNK), :] = (yacc_s[pl.ds(r0, CHUNK), :] + y).astype(BF16)

    scan(False, emit_fwd)
    scan(True, emit_bwd)


def _ssd(p3, dta, conv_w, conv_b, dskip, layer):
    Bsz, L, _ = p3.shape
    nc = L // CHUNK
    gw = HEADS_PER_GROUP * HEAD_DIM
    xs_blk, b_blk, c_blk = P_XS // gw, P_B // D_STATE, P_C // D_STATE
    cw_b0, cw_c0 = D_INNER // D_STATE, (D_INNER + GN) // D_STATE
    return pl.pallas_call(
        functools.partial(_ssd_body, nc=nc),
        grid=(Bsz, N_GROUPS),
        in_specs=[
            pl.BlockSpec((1, L, gw), lambda b, g: (b, 0, xs_blk + g)),
            pl.BlockSpec((1, L, D_STATE), lambda b, g: (b, 0, b_blk + g)),
            pl.BlockSpec((1, L, D_STATE), lambda b, g: (b, 0, c_blk + g)),
            pl.BlockSpec((DT_PAD, L), lambda b, g: (0, b)),
            pl.BlockSpec((None, CONV_W, gw), lambda b, g: (layer, 0, g)),
            pl.BlockSpec((None, CONV_W, D_STATE), lambda b, g: (layer, 0, cw_b0 + g)),
            pl.BlockSpec((None, CONV_W, D_STATE), lambda b, g: (layer, 0, cw_c0 + g)),
            pl.BlockSpec((None, 1, gw), lambda b, g: (layer, 0, g)),
            pl.BlockSpec((None, 1, D_STATE), lambda b, g: (layer, 0, cw_b0 + g)),
            pl.BlockSpec((None, 1, D_STATE), lambda b, g: (layer, 0, cw_c0 + g)),
            pl.BlockSpec((None, 1, gw), lambda b, g: (layer, 0, g)),
        ],
        out_specs=pl.BlockSpec((1, L, gw), lambda b, g: (b, 0, g)),
        out_shape=jax.ShapeDtypeStruct((Bsz, L, D_INNER), BF16),
        scratch_shapes=[
            pltpu.VMEM((L, gw), BF16),
            pltpu.VMEM((L, D_STATE), BF16),
            pltpu.VMEM((L, D_STATE), BF16),
            pltpu.VMEM((len(SIDE_TAPS) * CHUNK, CONV_WIN), BF16),
            pltpu.VMEM((2 * ROWS_PER_DIR, L), F32),
            pltpu.VMEM((L, gw), F32),
            pltpu.VMEM((D_STATE, gw), F32),
            pltpu.VMEM((2, HEADS_PER_GROUP, CHUNK, CHUNK + D_STATE), BF16),
            pltpu.VMEM((2, HEADS_PER_GROUP, D_STATE, CHUNK), BF16),
        ],
        compiler_params=pltpu.CompilerParams(
            dimension_semantics=("parallel", "parallel"),
            vmem_limit_bytes=VMEM_LIMIT_BYTES),
        name="ssd",
    )(p3, p3, p3, dta, conv_w, conv_w, conv_w, conv_b, conv_b, conv_b, dskip)


KV_ROWS = 3 * WINDOW
KV_DUP = 2 * KV_DIM
assert A_DIM ** -0.5 == 0.125


def _attn_body(sink_ref, q_ref, k_ref, v_ref, o_ref, bias_s, s_s, p_s, vt_s, *, layer, nb):
    i = pl.program_id(1)
    start = pl.multiple_of(jnp.clip(i - 1, 0, nb - 3) * WINDOW, WINDOW)
    heads_per_kv = A_HEADS // A_KV
    grp_lanes = heads_per_kv * WINDOW

    @pl.when((i <= 1) | (i == nb - 1))
    def _():
        off = i * WINDOW - start
        key = lax.broadcasted_iota(jnp.int32, (KV_ROWS, WINDOW), 0)
        qry = lax.broadcasted_iota(jnp.int32, (KV_ROWS, WINDOW), 1)
        dist = jnp.abs(off + qry - key)
        valid = dist <= WINDOW
        distf = dist.astype(F32)
        for h in range(A_HEADS):
            slope = 2.0 ** (-8.0 * (h + 1) / A_HEADS)
            bias_s[h // heads_per_kv, :, (h % heads_per_kv) * WINDOW:(h % heads_per_kv + 1) * WINDOW] = (
                jnp.where(valid, -slope * distf, NEG))

    low_q = lax.broadcasted_iota(jnp.int32, (WINDOW, LANES), 1) < A_DIM
    col_head = lax.broadcasted_iota(jnp.int32, (1, grp_lanes), 1) // WINDOW
    top = lax.broadcasted_iota(jnp.int32, (LANES, KV_ROWS), 0) < A_DIM
    trans_b = (((1,), (1,)), ((), ()))

    for g in range(A_KV):
        parts = []
        for pr in range(2 * g, 2 * g + 2):
            qp = q_ref[0, :, pr * LANES:(pr + 1) * LANES] * 0.125
            zero = jnp.zeros_like(qp)
            parts += [jnp.where(low_q, qp, zero), jnp.where(low_q, zero, qp)]
        kd = k_ref[0, pl.ds(start, KV_ROWS), g * LANES:(g + 1) * LANES]
        s_s[g] = lax.dot_general(kd, jnp.concatenate(parts, axis=0), trans_b, preferred_element_type=F32)
        vt = v_ref[0, pl.ds(start, KV_ROWS), g * LANES:(g + 1) * LANES].astype(F32).T
        vt_s[g] = jnp.where(top, vt, 1.0).astype(BF16)

    maxima, sinks = [], []
    for g in range(A_KV):
        sink = jnp.zeros((1, grp_lanes), F32)
        for e in range(heads_per_kv):
            sink = jnp.where(col_head == e, sink_ref[layer, g * heads_per_kv + e], sink)
        s = s_s[g] + bias_s[g]
        m = jnp.maximum(jnp.max(s, axis=0, keepdims=True), sink)
        p_s[g] = jnp.exp(s - m).astype(BF16)
        maxima.append(m)
        sinks.append(sink)

    for g in range(A_KV):
        res = jnp.dot(vt_s[g], p_s[g], preferred_element_type=F32)
        o_t = res[0:A_DIM] / (res[A_DIM:2 * A_DIM] + jnp.exp(sinks[g] - maxima[g]))
        o_grp = jnp.concatenate([o_t[:, e * WINDOW:(e + 1) * WINDOW] for e in range(heads_per_kv)], axis=0)
        o_ref[0, :, g * 2 * LANES:(g + 1) * 2 * LANES] = o_grp.T.astype(BF16)


def _attn(p3, sink, layer):
    Bsz, L, _ = p3.shape
    nb = L // WINDOW
    return pl.pallas_call(
        functools.partial(_attn_body, layer=layer, nb=nb),
        grid=(Bsz, nb),
        in_specs=[
            pl.BlockSpec(memory_space=pltpu.SMEM),
            pl.BlockSpec((1, WINDOW, Q_DIM), lambda b, i: (b, i, P_Q // Q_DIM)),
            pl.BlockSpec((1, L, KV_DUP), lambda b, i: (b, 0, P_K // KV_DUP)),
            pl.BlockSpec((1, L, KV_DUP), lambda b, i: (b, 0, P_V // KV_DUP)),
        ],
        out_specs=pl.BlockSpec((1, WINDOW, Q_DIM), lambda b, i: (b, i, 0)),
        out_shape=jax.ShapeDtypeStruct((Bsz, L, Q_DIM), BF16),
        scratch_shapes=[
            pltpu.VMEM((A_KV, KV_ROWS, 4 * WINDOW), F32),
            pltpu.VMEM((A_KV, KV_ROWS, 4 * WINDOW), F32),
            pltpu.VMEM((A_KV, KV_ROWS, 4 * WINDOW), BF16),
            pltpu.VMEM((A_KV, LANES, KV_ROWS), BF16),
        ],
        compiler_params=pltpu.CompilerParams(
            dimension_semantics=("parallel", "arbitrary"),
            vmem_limit_bytes=VMEM_LIMIT_BYTES),
        name="attn",
    )(sink, p3, p3, p3)


def _mix_body(y_ref, z_ref, at_ref, ga_ref, gb_ref, x_ref, nssd_ref, npost_ref,
              wssd_ref, wat_ref, wout_ref, o_ref):
    u = y_ref[...].astype(F32) * jax.nn.silu(z_ref[...].astype(F32))
    un = _rms(u, nssd_ref[...]).astype(BF16)
    y_ssd = jnp.dot(un, wssd_ref[...], preferred_element_type=F32)
    y_at = jnp.dot(at_ref[...], wat_ref[...], preferred_element_type=F32)
    mix_in = (jax.nn.sigmoid(ga_ref[...].astype(F32)) * y_ssd
              + jax.nn.sigmoid(gb_ref[...].astype(F32)) * y_at).astype(BF16)
    mix = jnp.dot(mix_in, wout_ref[...], preferred_element_type=F32)
    o_ref[...] = x_ref[...] + _rms(mix, npost_ref[...])


def _mix(y2, p2, at2, x2, n_ssd, n_post, w_ssd, w_at, w_out, layer, tm):
    T = x2.shape[0]
    row = lambda blk: (lambda i: (i, blk))
    par = lambda i: (layer, 0, 0)
    return pl.pallas_call(
        _mix_body,
        grid=(T // tm,),
        in_specs=[
            pl.BlockSpec((tm, D_INNER), row(0)),
            pl.BlockSpec((tm, D_INNER), row(P_Z // D_INNER)),
            pl.BlockSpec((tm, Q_DIM), row(0)),
            pl.BlockSpec((tm, D_MODEL), row(P_GA // D_MODEL)),
            pl.BlockSpec((tm, D_MODEL), row(P_GB // D_MODEL)),
            pl.BlockSpec((tm, D_MODEL), row(0)),
            pl.BlockSpec((None, 1, D_INNER), par),
            pl.BlockSpec((None, 1, D_MODEL), par),
            pl.BlockSpec((None, D_INNER, D_MODEL), par),
            pl.BlockSpec((None, Q_DIM, D_MODEL), par),
            pl.BlockSpec((None, D_MODEL, D_MODEL), par),
        ],
        out_specs=pl.BlockSpec((tm, D_MODEL), row(0)),
        out_shape=jax.ShapeDtypeStruct((T, D_MODEL), F32),
        compiler_params=pltpu.CompilerParams(
            dimension_semantics=("parallel",),
            vmem_limit_bytes=VMEM_LIMIT_BYTES),
        name="mix",
    )(y2, p2, at2, p2, p2, x2, n_ssd, n_post, w_ssd, w_at, w_out)


FF_CHUNK = D_FF // 2


def _ffn_body(x_ref, npre_ref, npost_ref, win_ref, wout_ref, o_ref):
    x = x_ref[...]
    h = _rms(x, npre_ref[...]).astype(BF16)
    acc = jnp.zeros(x.shape, F32)
    for c0 in range(0, D_FF, FF_CHUNK):
        gate = jnp.dot(h, win_ref[:, c0:c0 + FF_CHUNK], preferred_element_type=F32)
        up = jnp.dot(h, win_ref[:, D_FF + c0:D_FF + c0 + FF_CHUNK], preferred_element_type=F32)
        act = (jax.nn.silu(gate) * up).astype(BF16)
        acc = acc + jnp.dot(act, wout_ref[c0:c0 + FF_CHUNK, :], preferred_element_type=F32)
    o_ref[...] = x + _rms(acc, npost_ref[...])


def _ffn(x2, n_pre, n_post, w_in, w_out, layer, tm):
    T = x2.shape[0]
    par = lambda i: (layer, 0, 0)
    return pl.pallas_call(
        _ffn_body,
        grid=(T // tm,),
        in_specs=[
            pl.BlockSpec((tm, D_MODEL), lambda i: (i, 0)),
            pl.BlockSpec((None, 1, D_MODEL), par),
            pl.BlockSpec((None, 1, D_MODEL), par),
            pl.BlockSpec((None, D_MODEL, 2 * D_FF), par),
            pl.BlockSpec((None, D_FF, D_MODEL), par),
        ],
        out_specs=pl.BlockSpec((tm, D_MODEL), lambda i: (i, 0)),
        out_shape=jax.ShapeDtypeStruct((T, D_MODEL), F32),
        compiler_params=pltpu.CompilerParams(
            dimension_semantics=("parallel",),
            vmem_limit_bytes=VMEM_LIMIT_BYTES),
        name="ffn",
    )(x2, n_pre, n_post, w_in, w_out)


def _tile(total, want):
    t = min(total, want)
    assert total % t == 0, (total, want)
    return t


def kernel(x, pre_mix_norm, w_in, conv_w, conv_b, dt_bias_f, dt_bias_b, A_log_f, A_log_b, D_skip,
           ssd_norm, w_ssd_proj, attn_sink, w_attn_proj, w_out, post_mix_norm, pre_ffn_norm,
           w_ffn_in, w_ffn_out, post_ffn_norm):
    Bsz, L, D = x.shape
    depth = w_in.shape[0]
    assert D == D_MODEL and L % (4 * CHUNK) == 0 and L >= 3 * WINDOW
    T = Bsz * L

    sizes = dict(z=D_INNER, xs=D_INNER, b=GN, c=GN, dt=2 * N_HEADS, q=Q_DIM, k=KV_DIM, v=KV_DIM,
                 ga=D_MODEL, gb=D_MODEL)
    cols, start = {}, 0
    for name, size in sizes.items():
        cols[name] = w_in[:, :, start:start + size]
        start += size
    assert start == w_in.shape[-1]
    for name in ("k", "v"):
        per_head = cols[name].reshape(depth, D, A_KV, 1, A_DIM)
        cols[name] = jnp.broadcast_to(per_head, (depth, D, A_KV, 2, A_DIM)).reshape(depth, D, 2 * KV_DIM)
    w_main = jnp.concatenate(
        [cols[n] for n in ("z", "xs", "q", "ga", "gb", "b", "c", "k", "v")], axis=-1).astype(BF16)
    assert w_main.shape[-1] == NP
    w_dt = jnp.concatenate([cols["dt"], cols["dt"]], axis=-1).astype(BF16)
    assert w_dt.shape[-1] == DT_PAD
    bias = jnp.concatenate([dt_bias_f, dt_bias_b], axis=-1)
    head_params = jnp.stack(
        [jnp.concatenate([bias, bias], axis=-1),
         jnp.concatenate([jnp.zeros_like(bias), A_log_f, A_log_b], axis=-1)]
        + [jnp.zeros((depth, LANES), F32)] * 6, axis=1)
    dskip = jnp.repeat(D_skip, HEAD_DIM, axis=-1)[:, None, :]
    vec = lambda a: a[:, None, :]
    conv_b3 = vec(conv_b)
    w_ssd_b, w_at_b, w_out_b = (w.astype(BF16) for w in (w_ssd_proj, w_attn_proj, w_out))
    w_fin_b, w_fout_b = w_ffn_in.astype(BF16), w_ffn_out.astype(BF16)

    tm_in = _tile(T, 2048)
    tm_mix = _tile(T, 512)
    tm_ffn = _tile(T, 512)

    x2 = x.reshape(T, D)
    for i in range(depth):
        p2, dt2 = _in_proj(x2, vec(pre_mix_norm), w_main, w_dt, head_params, i, tm_in, 1024)
        p3 = p2.reshape(Bsz, L, NP)
        y3 = _ssd(p3, dt2, conv_w, conv_b3, dskip, i)
        at3 = _attn(p3, attn_sink, i)
        x2 = _mix(y3.reshape(T, D_INNER), p2, at3.reshape(T, Q_DIM), x2, vec(ssd_norm),
                  vec(post_mix_norm), w_ssd_b, w_at_b, w_out_b, i, tm_mix)
        x2 = _ffn(x2, vec(pre_ffn_norm), vec(post_ffn_norm), w_fin_b, w_fout_b, i, tm_ffn)
    return x2.reshape(Bsz, L, D)
```
